```python
import math
import jax, jax.numpy as jnp
from jax import lax
import numpy as np

D_MODEL = 1024
BATCH = 2
SEQ = 16384
DEPTH = 4

N_MIXERS = 3
SB_HEAD_DIM = 128
SB_HEADS = D_MODEL // SB_HEAD_DIM
DIFF_HALF_DIM = 128
DIFF_HEADS = D_MODEL // (2 * DIFF_HALF_DIM)
MEM_HEAD_DIM = 64
MEM_HEADS = 4
MEM_LEN = 256
MEM_WIDTH = MEM_HEADS * MEM_HEAD_DIM
MIX_WIDTH = D_MODEL
IN_WIDTH = 3 * MIX_WIDTH + MEM_WIDTH
OUT_IN_WIDTH = MIX_WIDTH + MEM_WIDTH
CONV_WIDTH = 3
D_FF = -(-8 * D_MODEL // (3 * 256)) * 256
BLOCK_Q = 128
N_DIFF = (DEPTH + 1) // 3
N_CONV = DEPTH // 3
NORM_EPS = 1e-6
HEAD_NORM_EPS = 1e-5
LAMBDA_STD = 0.1

kernel_name = "hybrid_sb_diff_conv_mem_trunk"


def _rmsnorm(x, g, eps=NORM_EPS):
    xf = x.astype(jnp.float32)
    y = xf * lax.rsqrt(jnp.mean(xf * xf, axis=-1, keepdims=True) + eps)
    return (y * g.astype(jnp.float32)).astype(x.dtype)


def _heads(t, n_heads, d):
    b, s, _ = t.shape
    return t.reshape(b, s, n_heads, d).transpose(0, 2, 1, 3)


def _rev_cumsum(u):
    shp = u.shape
    nc = shp[-1] // BLOCK_Q
    uc = u.reshape(shp[:-1] + (nc, BLOCK_Q))
    r = jnp.arange(BLOCK_Q)
    tri = (r[:, None] >= r[None, :]).astype(u.dtype)
    within = jnp.einsum('...cj,js->...cs', uc, tri, precision=lax.Precision.HIGHEST)
    c = jnp.arange(nc)
    later = (c[:, None] > c[None, :]).astype(u.dtype)
    suffix = jnp.einsum('...k,kc->...c', within[..., 0], later,
                        precision=lax.Precision.HIGHEST)
    return (within + suffix[..., None]).reshape(shp)


def _stick_breaking(cols):
    b, seq, _ = cols.shape
    q = _heads(cols[..., :MIX_WIDTH], SB_HEADS, SB_HEAD_DIM) * (SB_HEAD_DIM ** -0.5)
    k = _heads(cols[..., MIX_WIDTH:2 * MIX_WIDTH], SB_HEADS, SB_HEAD_DIM)
    v = _heads(cols[..., 2 * MIX_WIDTH:], SB_HEADS, SB_HEAD_DIM)
    outs = []
    for i in range(seq // BLOCK_Q):
        end = (i + 1) * BLOCK_Q
        qb = q[:, :, i * BLOCK_Q:end]
        z = jnp.einsum('bhqd,bhkd->bhqk', qb, k[:, :, :end]).astype(jnp.float32)
        qpos = i * BLOCK_Q + jnp.arange(BLOCK_Q)
        mask = jnp.arange(end)[None, :] < qpos[:, None]
        u = jnp.where(mask, jax.nn.softplus(z), 0.0)
        a = jnp.where(mask, jnp.exp(z - _rev_cumsum(u)), 0.0)
        outs.append(jnp.einsum('bhqk,bhkd->bhqd', a.astype(v.dtype), v[:, :, :end]))
    o = jnp.concatenate(outs, axis=2)
    return o.transpose(0, 2, 1, 3).reshape(b, seq, MIX_WIDTH)


def _diff_attention(cols, lam_q1, lam_k1, lam_q2, lam_k2, g_head, layer_idx):
    b, seq, _ = cols.shape
    scale = DIFF_HALF_DIM ** -0.5
    qh = cols[..., :MIX_WIDTH].reshape(b, seq, DIFF_HEADS, 2, DIFF_HALF_DIM) * scale
    kh = cols[..., MIX_WIDTH:2 * MIX_WIDTH].reshape(b, seq, DIFF_HEADS, 2, DIFF_HALF_DIM)
    q1 = qh[..., 0, :].transpose(0, 2, 1, 3)
    q2 = qh[..., 1, :].transpose(0, 2, 1, 3)
    k1 = kh[..., 0, :].transpose(0, 2, 1, 3)
    k2 = kh[..., 1, :].transpose(0, 2, 1, 3)
    v = _heads(cols[..., 2 * MIX_WIDTH:], DIFF_HEADS, 2 * DIFF_HALF_DIM)

    lambda_init = 0.8 - 0.6 * math.exp(-0.3 * layer_idx)
    f32 = jnp.float32
    lam = (jnp.exp(jnp.sum(lam_q1.astype(f32) * lam_k1.astype(f32)))
           - jnp.exp(jnp.sum(lam_q2.astype(f32) * lam_k2.astype(f32))) + lambda_init)
    slopes = 2.0 ** (-8.0 * jnp.arange(1, DIFF_HEADS + 1, dtype=f32) / DIFF_HEADS)

    outs = []
    for i in range(seq // BLOCK_Q):
        end = (i + 1) * BLOCK_Q
        qpos = i * BLOCK_Q + jnp.arange(BLOCK_Q)
        dist = (qpos[:, None] - jnp.arange(end)[None, :]).astype(f32)
        mask = dist >= 0.0
        bias = -slopes[:, None, None] * dist

        def probs(qq, kk):
            s = jnp.einsum('bhqd,bhkd->bhqk', qq[:, :, i * BLOCK_Q:end],
                           kk[:, :, :end]).astype(f32) + bias
            return jax.nn.softmax(jnp.where(mask, s, -jnp.inf), axis=-1)

        a = probs(q1, k1) - lam * probs(q2, k2)
        outs.append(jnp.einsum('bhqk,bhkd->bhqd', a.astype(v.dtype), v[:, :, :end]))
    o = jnp.concatenate(outs, axis=2).transpose(0, 2, 1, 3)
    o = _rmsnorm(o, g_head, HEAD_NORM_EPS) * (1.0 - lambda_init)
    return o.reshape(b, seq, MIX_WIDTH)


def _short_conv(cols, conv_w):
    gate_b = cols[..., :MIX_WIDTH]
    gate_c = cols[..., MIX_WIDTH:2 * MIX_WIDTH]
    h = cols[..., 2 * MIX_WIDTH:]
    u = gate_c * h
    y = lax.conv_general_dilated(
        u, conv_w[:, None, :].astype(u.dtype),
        window_strides=(1,), padding=[(CONV_WIDTH - 1, 0)],
        dimension_numbers=('NWC', 'WIO', 'NWC'),
        feature_group_count=MIX_WIDTH)
    return gate_b * y


def _memory_attention(mq, mem_n, w_kv):
    b, seq, _ = mq.shape
    kv = mem_n @ w_kv
    km = kv[..., :MEM_WIDTH].reshape(b, MEM_LEN, MEM_HEADS, MEM_HEAD_DIM)
    vm = kv[..., MEM_WIDTH:].reshape(b, MEM_LEN, MEM_HEADS, MEM_HEAD_DIM)
    q = mq.reshape(b, seq, MEM_HEADS, MEM_HEAD_DIM)
    s = jnp.einsum('bshd,bmhd->bhsm', q, km).astype(jnp.float32) * MEM_HEAD_DIM ** -0.5
    p = jax.nn.softmax(s, axis=-1)
    o = jnp.einsum('bhsm,bmhd->bshd', p.astype(vm.dtype), vm)
    return o.reshape(b, seq, MEM_WIDTH)


def _swiglu(x, w_in, w_out):
    gu = x @ w_in
    return (jax.nn.silu(gu[..., :D_FF]) * gu[..., D_FF:]) @ w_out


def setup_inputs(seed: int = 0) -> dict:
    key = jax.random.key(seed)
    ks = jax.random.split(key, 17)
    f32 = jnp.float32
    n = lambda k, shape, s: jax.random.normal(k, shape, f32) * s
    return {
        "x": n(ks[0], (BATCH, SEQ, D_MODEL), 1.0),
        "mem": n(ks[1], (BATCH, MEM_LEN, D_MODEL), 1.0),
        "g_mix": 1.0 + n(ks[2], (DEPTH, D_MODEL), 0.02),
        "w_in": n(ks[3], (DEPTH, D_MODEL, IN_WIDTH), D_MODEL ** -0.5),
        "w_mem_kv": n(ks[4], (DEPTH, D_MODEL, 2 * MEM_WIDTH), D_MODEL ** -0.5),
        "w_o": n(ks[5], (DEPTH, OUT_IN_WIDTH, D_MODEL), OUT_IN_WIDTH ** -0.5),
        "g_ffn": 1.0 + n(ks[6], (DEPTH, D_MODEL), 0.02),
        "w_ffn_in": n(ks[7], (DEPTH, D_MODEL, 2 * D_FF), D_MODEL ** -0.5),
        "w_ffn_out": n(ks[8], (DEPTH, D_FF, D_MODEL), D_FF ** -0.5),
        "lam_q1": n(ks[9], (N_DIFF, DIFF_HALF_DIM), LAMBDA_STD),
        "lam_k1": n(ks[10], (N_DIFF, DIFF_HALF_DIM), LAMBDA_STD),
        "lam_q2": n(ks[11], (N_DIFF, DIFF_HALF_DIM), LAMBDA_STD),
        "lam_k2": n(ks[12], (N_DIFF, DIFF_HALF_DIM), LAMBDA_STD),
        "g_diff_head": 1.0 + n(ks[13], (N_DIFF, 2 * DIFF_HALF_DIM), 0.02),
        "conv_w": n(ks[14], (N_CONV, CONV_WIDTH, D_MODEL), CONV_WIDTH ** -0.5),
        "g_mem": 1.0 + n(ks[15], (D_MODEL,), 0.02),
        "g_final": 1.0 + n(ks[16], (D_MODEL,), 0.02),
    }


def reference(x, mem, g_mix, w_in, w_mem_kv, w_o, g_ffn, w_ffn_in, w_ffn_out,
              lam_q1, lam_k1, lam_q2, lam_k2, g_diff_head, conv_w, g_mem, g_final):
    mem_n = _rmsnorm(mem, g_mem)
    for i in range(DEPTH):
        kind = i % N_MIXERS
        j = i // N_MIXERS
        hn = _rmsnorm(x, g_mix[i])
        proj = hn @ w_in[i]
        cols, mq = proj[..., :3 * MIX_WIDTH], proj[..., 3 * MIX_WIDTH:]
        if kind == 0:
            mix = _stick_breaking(cols)
        elif kind == 1:
            mix = _diff_attention(cols, lam_q1[j], lam_k1[j], lam_q2[j], lam_k2[j],
                                  g_diff_head[j], i)
        else:
            mix = _short_conv(cols, conv_w[j])
        mo = _memory_attention(mq, mem_n, w_mem_kv[i])
        x = x + jnp.concatenate([mix, mo], axis=-1) @ w_o[i]
        x = x + _swiglu(_rmsnorm(x, g_ffn[i]), w_ffn_in[i], w_ffn_out[i])
    return _rmsnorm(x, g_final)
```

```python
import functools
import math

import jax
import jax.numpy as jnp
from jax import lax
from jax.experimental import pallas as pl
from jax.experimental.pallas import tpu as pltpu

F32 = jnp.float32
BF16 = jnp.bfloat16

N_MIXERS = 3
HEAD_DIM = 128
DIFF_V_DIM = 2 * HEAD_DIM
MEM_HEADS = 4
CONV_WIDTH = 3
NORM_EPS = 1e-6
HEAD_NORM_EPS = 1e-5
MASK_VALUE = -1e30
SOFTPLUS_LINEAR_ABOVE = 20.0

V7X_VMEM_BYTES = 64 * 1024 * 1024
SUBLANES = 8


def _vmem_limit(estimate_bytes):
    return int(min(max(estimate_bytes, 16 * 1024 * 1024), V7X_VMEM_BYTES - 8 * 1024 * 1024))


def _rms(xf, g, eps):
    ms = jnp.mean(xf * xf, axis=-1, keepdims=True)
    return xf * lax.rsqrt(ms + eps) * g


def _dot(a, b):
    return jnp.dot(a, b, preferred_element_type=F32)


def _dot_nt(a, b):
    return lax.dot_general(a, b, (((1,), (1,)), ((), ())), preferred_element_type=F32)


def _const_spec(shape):
    nd = len(shape)
    return pl.BlockSpec(shape, lambda *_: (0,) * nd, pipeline_mode=pl.Buffered(1))


def _mem_kv_kernel(mem_ref, g_ref, w_ref, k_ref, v_ref, *, mem_width):
    mem_n = _rms(mem_ref[...], g_ref[...], NORM_EPS).astype(BF16)
    kv = _dot(mem_n, w_ref[0])
    head_dim = mem_width // MEM_HEADS
    k_ref[0] = (kv[:, :mem_width] * head_dim ** -0.5).astype(BF16)
    v_ref[0] = kv[:, mem_width:].astype(BF16)


def _mem_kv(mem2d, g_mem, w_kv):
    depth, d_model, two_w = w_kv.shape
    mem_width = two_w // 2
    rows = mem2d.shape[0]
    return pl.pallas_call(
        functools.partial(_mem_kv_kernel, mem_width=mem_width),
        grid=(depth,),
        in_specs=[
            pl.BlockSpec((rows, d_model), lambda i: (0, 0)),
            pl.BlockSpec((1, d_model), lambda i: (0, 0)),
            pl.BlockSpec((1, d_model, two_w), lambda i: (i, 0, 0)),
        ],
        out_specs=[
            pl.BlockSpec((1, rows, mem_width), lambda i: (i, 0, 0)),
            pl.BlockSpec((1, rows, mem_width), lambda i: (i, 0, 0)),
        ],
        out_shape=[jax.ShapeDtypeStruct((depth, rows, mem_width), BF16)] * 2,
        name="mem_kv",
    )(mem2d, g_mem.reshape(1, d_model), w_kv)


def _in_proj_kernel(x_ref, g_ref, w_ref, q_ref, k_ref, v_ref, mq_ref, *, q_scale):
    d = x_ref.shape[1]
    hn = _rms(x_ref[...], g_ref[...], NORM_EPS).astype(BF16)
    q_ref[...] = (_dot(hn, w_ref[:, 0:d]) * q_scale).astype(BF16)
    k_ref[...] = _dot(hn, w_ref[:, d:2 * d]).astype(BF16)
    v_ref[...] = _dot(hn, w_ref[:, 2 * d:3 * d]).astype(BF16)
    mq_ref[...] = _dot(hn, w_ref[:, 3 * d:]).astype(BF16)


def _in_proj(x2d, g, w, *, q_scale, tm):
    t, d = x2d.shape
    n = w.shape[1]
    mw = n - 3 * d
    est = 2 * tm * d * 4 + d * n * 2 + 2 * tm * n * 2 + 3 * tm * d * 4
    return pl.pallas_call(
        functools.partial(_in_proj_kernel, q_scale=q_scale),
        grid=(t // tm,),
        in_specs=[
            pl.BlockSpec((tm, d), lambda i: (i, 0)),
            _const_spec((1, d)),
            _const_spec((d, n)),
        ],
        out_specs=[
            pl.BlockSpec((tm, d), lambda i: (i, 0)),
            pl.BlockSpec((tm, d), lambda i: (i, 0)),
            pl.BlockSpec((tm, d), lambda i: (i, 0)),
            pl.BlockSpec((tm, mw), lambda i: (i, 0)),
        ],
        out_shape=[jax.ShapeDtypeStruct((t, d), BF16)] * 3
        + [jax.ShapeDtypeStruct((t, mw), BF16)],
        compiler_params=pltpu.CompilerParams(
            dimension_semantics=("arbitrary",), vmem_limit_bytes=_vmem_limit(est)),
        name="in_proj",
    )(x2d, g.reshape(1, d), w)


def _conv_proj_kernel(x_ref, g_ref, w_ref, cw_ref, mix_ref, mq_ref, u_ref, *, tiles_per_seq):
    tm, d = x_ref.shape
    i = pl.program_id(0)

    @pl.when(i % tiles_per_seq == 0)
    def _():
        u_ref[0:SUBLANES, :] = jnp.zeros((SUBLANES, d), F32)

    hn = _rms(x_ref[...], g_ref[...], NORM_EPS).astype(BF16)
    gate_c = _dot(hn, w_ref[:, d:2 * d])
    h = _dot(hn, w_ref[:, 2 * d:3 * d])
    u_ref[SUBLANES:SUBLANES + tm, :] = gate_c * h
    y = cw_ref[CONV_WIDTH - 1:CONV_WIDTH, :] * u_ref[SUBLANES:SUBLANES + tm, :]
    for tap in range(CONV_WIDTH - 1):
        shift = CONV_WIDTH - 1 - tap
        y = y + cw_ref[tap:tap + 1, :] * u_ref[SUBLANES - shift:SUBLANES - shift + tm, :]
    gate_b = _dot(hn, w_ref[:, 0:d])
    mix_ref[...] = (gate_b * y).astype(BF16)
    mq_ref[...] = _dot(hn, w_ref[:, 3 * d:]).astype(BF16)
    u_ref[0:SUBLANES, :] = u_ref[tm:tm + SUBLANES, :]


def _conv_proj(x2d, g, w, conv_w, *, seq, tm):
    t, d = x2d.shape
    n = w.shape[1]
    mw = n - 3 * d
    est = 2 * tm * d * 4 + d * n * 2 + 2 * tm * (d + mw) * 2 + 5 * tm * d * 4
    return pl.pallas_call(
        functools.partial(_conv_proj_kernel, tiles_per_seq=seq // tm),
        grid=(t // tm,),
        in_specs=[
            pl.BlockSpec((tm, d), lambda i: (i, 0)),
            _const_spec((1, d)),
            _const_spec((d, n)),
            _const_spec((CONV_WIDTH, d)),
        ],
        out_specs=[
            pl.BlockSpec((tm, d), lambda i: (i, 0)),
            pl.BlockSpec((tm, mw), lambda i: (i, 0)),
        ],
        out_shape=[jax.ShapeDtypeStruct((t, d), BF16), jax.ShapeDtypeStruct((t, mw), BF16)],
        scratch_shapes=[pltpu.VMEM((tm + SUBLANES, d), F32)],
        compiler_params=pltpu.CompilerParams(
            dimension_semantics=("arbitrary",), vmem_limit_bytes=_vmem_limit(est)),
        name="conv_proj",
    )(x2d, g.reshape(1, d), w, conv_w)


def _sb_kernel(q_ref, k_ref, v_ref, tri_ref, o_ref, *, blk):
    i = pl.program_id(2)
    q = q_ref[0]
    row = lax.broadcasted_iota(jnp.int32, (blk, blk), 0)
    col = lax.broadcasted_iota(jnp.int32, (blk, blk), 1)
    below_diag = col < row

    def block(j, acc, c, mask):
        start = pl.multiple_of(j * blk, blk)
        ks = k_ref[0, pl.ds(start, blk), :]
        vs = v_ref[0, pl.ds(start, blk), :]
        z = _dot_nt(q, ks)
        u = jnp.where(z > SOFTPLUS_LINEAR_ABOVE, z, jnp.log(1.0 + jnp.exp(z)))
        if mask is not None:
            u = jnp.where(mask, u, 0.0)
        u_hi = u.astype(BF16)
        u_lo = (u - u_hi.astype(F32)).astype(BF16)
        suffix = _dot(jnp.concatenate([u_hi, u_lo], axis=1), tri_ref[...])
        a = jnp.exp(z - suffix - c)
        if mask is not None:
            a = jnp.where(mask, a, 0.0)
        acc = acc + _dot(a.astype(BF16), vs)
        c = c + jnp.sum(u, axis=1, keepdims=True)
        return acc, c

    acc0 = jnp.zeros((blk, q.shape[1]), F32)
    c0 = jnp.zeros((blk, 1), F32)
    acc, c = block(i, acc0, c0, below_diag)

    def body(jj, carry):
        return block(i - 1 - jj, carry[0], carry[1], None)

    acc, c = lax.fori_loop(0, i, body, (acc, c))
    o_ref[0] = acc.astype(o_ref.dtype)


def _sb_attention(q, k, v, *, blk):
    b, s, d = q.shape
    heads = d // HEAD_DIM
    r = jnp.arange(blk)
    tri = (r[:, None] >= r[None, :]).astype(BF16)
    tri2 = jnp.concatenate([tri, tri], axis=0)
    est = 2 * 2 * s * HEAD_DIM * 2 + 12 * blk * blk * 4
    return pl.pallas_call(
        functools.partial(_sb_kernel, blk=blk),
        grid=(b, heads, s // blk),
        in_specs=[
            pl.BlockSpec((1, blk, HEAD_DIM), lambda bi, h, i: (bi, i, h)),
            pl.BlockSpec((1, s, HEAD_DIM), lambda bi, h, i: (bi, 0, h)),
            pl.BlockSpec((1, s, HEAD_DIM), lambda bi, h, i: (bi, 0, h)),
            _const_spec((2 * blk, blk)),
        ],
        out_specs=pl.BlockSpec((1, blk, HEAD_DIM), lambda bi, h, i: (bi, i, h)),
        out_shape=jax.ShapeDtypeStruct((b, s, d), BF16),
        compiler_params=pltpu.CompilerParams(
            dimension_semantics=("arbitrary", "arbitrary", "arbitrary"),
            vmem_limit_bytes=_vmem_limit(est)),
        name="sb_attention",
    )(q, k, v, tri2)


def _diff_kernel(slopes_ref, lq1_ref, lk1_ref, lq2_ref, lk2_ref, g_ref,
                 q1_ref, q2_ref, k1_ref, k2_ref, v_ref, o_ref, *, blk, lambda_init):
    h = pl.program_id(1)
    i = pl.program_id(2)
    slope = slopes_ref[h]
    q1 = q1_ref[0]
    q2 = q2_ref[0]
    row = lax.broadcasted_iota(jnp.int32, (blk, blk), 0)
    col = lax.broadcasted_iota(jnp.int32, (blk, blk), 1)
    causal = col <= row
    local_bias = (col - row).astype(F32) * slope

    def one_map(qx, ks, vs, bias, mask, m, l, acc):
        s = _dot_nt(qx, ks) + bias
        if mask is not None:
            s = jnp.where(mask, s, MASK_VALUE)
        m_new = jnp.maximum(m, jnp.max(s, axis=1, keepdims=True))
        alpha = jnp.exp(m - m_new)
        p = jnp.exp(s - m_new)
        l = alpha * l + jnp.sum(p, axis=1, keepdims=True)
        acc = alpha * acc + _dot(p.astype(BF16), vs)
        return m_new, l, acc

    def block(j, carry, mask):
        start = pl.multiple_of(j * blk, blk)
        vs = v_ref[0, pl.ds(start, blk), :]
        bias = local_bias - slope * ((i - j) * blk).astype(F32)
        st1 = one_map(q1, k1_ref[0, pl.ds(start, blk), :], vs, bias, mask, *carry[0:3])
        st2 = one_map(q2, k2_ref[0, pl.ds(start, blk), :], vs, bias, mask, *carry[3:6])
        return st1 + st2

    m0 = jnp.full((blk, 1), MASK_VALUE, F32)
    l0 = jnp.zeros((blk, 1), F32)
    a0 = jnp.zeros((blk, v_ref.shape[2]), F32)
    carry = block(i, (m0, l0, a0, m0, l0, a0), causal)
    carry = lax.fori_loop(0, i, lambda jj, cr: block(i - 1 - jj, cr, None), carry)
    _, l1, acc1, _, l2, acc2 = carry

    lam = (jnp.exp(jnp.sum(lq1_ref[...] * lk1_ref[...], axis=1, keepdims=True))
           - jnp.exp(jnp.sum(lq2_ref[...] * lk2_ref[...], axis=1, keepdims=True))
           + lambda_init)
    o = acc1 / l1 - lam * (acc2 / l2)
    o = _rms(o, g_ref[...], HEAD_NORM_EPS) * (1.0 - lambda_init)
    o_ref[0] = o.astype(o_ref.dtype)


def _diff_attention(q, k, v, lq1, lk1, lq2, lk2, g_head, *, layer_idx, blk):
    b, s, d = q.shape
    heads = d // DIFF_V_DIM
    lambda_init = 0.8 - 0.6 * math.exp(-0.3 * layer_idx)
    slopes = 2.0 ** (-8.0 * jnp.arange(1, heads + 1, dtype=F32) / heads)
    vec = lambda a: a.reshape(1, -1).astype(F32)
    est = 2 * 2 * s * (2 * HEAD_DIM + DIFF_V_DIM) * 2 + 16 * blk * blk * 4
    return pl.pallas_call(
        functools.partial(_diff_kernel, blk=blk, lambda_init=lambda_init),
        grid=(b, heads, s // blk),
        in_specs=[
            pl.BlockSpec(memory_space=pltpu.SMEM),
            _const_spec((1, HEAD_DIM)), _const_spec((1, HEAD_DIM)),
            _const_spec((1, HEAD_DIM)), _const_spec((1, HEAD_DIM)),
            _const_spec((1, DIFF_V_DIM)),
            pl.BlockSpec((1, blk, HEAD_DIM), lambda bi, h, i: (bi, i, 2 * h)),
            pl.BlockSpec((1, blk, HEAD_DIM), lambda bi, h, i: (bi, i, 2 * h + 1)),
            pl.BlockSpec((1, s, HEAD_DIM), lambda bi, h, i: (bi, 0, 2 * h)),
            pl.BlockSpec((1, s, HEAD_DIM), lambda bi, h, i: (bi, 0, 2 * h + 1)),
            pl.BlockSpec((1, s, DIFF_V_DIM), lambda bi, h, i: (bi, 0, h)),
        ],
        out_specs=pl.BlockSpec((1, blk, DIFF_V_DIM), lambda bi, h, i: (bi, i, h)),
        out_shape=jax.ShapeDtypeStruct((b, s, d), BF16),
        compiler_params=pltpu.CompilerParams(
            dimension_semantics=("arbitrary", "arbitrary", "arbitrary"),
            vmem_limit_bytes=_vmem_limit(est)),
        name="diff_attention",
    )(slopes, vec(lq1), vec(lk1), vec(lq2), vec(lk2), vec(g_head), q, q, k, k, v)


def _post_kernel(mix_ref, mq_ref, x_ref, km_ref, vm_ref, wo_ref, g_ref, w1_ref, w2_ref, gf_ref,
                 o_ref, *, ff_chunks, final_norm):
    tm, d = x_ref.shape
    mw = mq_ref.shape[1]
    d_ff = w2_ref.shape[0]
    head_dim = mw // MEM_HEADS

    mq = mq_ref[...]
    km = km_ref[0]
    vm = vm_ref[0]
    lane_head = lax.broadcasted_iota(jnp.int32, (tm, mw), 1) // head_dim
    mo = jnp.zeros((tm, mw), F32)
    for hd in range(MEM_HEADS):
        in_head = lane_head == hd
        s = _dot_nt(jnp.where(in_head, mq, jnp.zeros_like(mq)), km)
        p = jnp.exp(s - jnp.max(s, axis=1, keepdims=True))
        l = jnp.sum(p, axis=1, keepdims=True)
        mo = jnp.where(in_head, _dot(p.astype(BF16), vm) / l, mo)

    y = x_ref[...] + _dot(mix_ref[...], wo_ref[0:d, :]) + _dot(mo.astype(BF16), wo_ref[d:, :])

    yn = _rms(y, g_ref[...], NORM_EPS).astype(BF16)
    hidden = []
    for lo, hi in ff_chunks:
        gate = _dot(yn, w1_ref[:, lo:hi])
        up = _dot(yn, w1_ref[:, d_ff + lo:d_ff + hi])
        hidden.append((gate * (1.0 / (1.0 + jnp.exp(-gate))) * up).astype(BF16))
    y = y + _dot(jnp.concatenate(hidden, axis=1), w2_ref[...])
    if final_norm:
        y = _rms(y, gf_ref[...], NORM_EPS)
    o_ref[...] = y


def _ff_chunks(d_ff, n_chunks, align):
    tiles = d_ff // align
    bounds = [align * ((tiles * c) // n_chunks) for c in range(n_chunks)] + [d_ff]
    return tuple((bounds[c], bounds[c + 1]) for c in range(n_chunks))


def _post(mix2d, mq2d, x2d, km, vm, wo, g_ffn, w1, w2, g_final, *, seq, tm, final_norm):
    t, d = x2d.shape
    mw = mq2d.shape[1]
    mem_len = km.shape[1]
    d_ff = w2.shape[0]
    tiles_per_seq = seq // tm
    ff_chunks = _ff_chunks(d_ff, 2, 256)
    chunk = max(hi - lo for lo, hi in ff_chunks)
    est = ((wo.size + w1.size + w2.size) * 2 + 2 * tm * (d + mw) * 2 + 4 * tm * d * 4
           + tm * chunk * 12 + 4 * tm * d * 4)
    return pl.pallas_call(
        functools.partial(_post_kernel, ff_chunks=ff_chunks, final_norm=final_norm),
        grid=(t // tm,),
        in_specs=[
            pl.BlockSpec((tm, d), lambda i: (i, 0)),
            pl.BlockSpec((tm, mw), lambda i: (i, 0)),
            pl.BlockSpec((tm, d), lambda i: (i, 0)),
            pl.BlockSpec((1, mem_len, mw), lambda i: (i // tiles_per_seq, 0, 0)),
            pl.BlockSpec((1, mem_len, mw), lambda i: (i // tiles_per_seq, 0, 0)),
            _const_spec(wo.shape),
            _const_spec((1, d)),
            _const_spec(w1.shape),
            _const_spec(w2.shape),
            _const_spec((1, d)),
        ],
        out_specs=pl.BlockSpec((tm, d), lambda i: (i, 0)),
        out_shape=jax.ShapeDtypeStruct((t, d), F32),
        compiler_params=pltpu.CompilerParams(
            dimension_semantics=("arbitrary",), vmem_limit_bytes=_vmem_limit(est)),
        name="post",
    )(mix2d, mq2d, x2d, km, vm, wo, g_ffn.reshape(1, d), w1, w2, g_final.reshape(1, d))


def kernel(x, mem, g_mix, w_in, w_mem_kv, w_o, g_ffn, w_ffn_in, w_ffn_out,
           lam_q1, lam_k1, lam_q2, lam_k2, g_diff_head, conv_w, g_mem, g_final):
    b, seq, d = x.shape
    mem_len = mem.shape[1]
    depth = w_in.shape[0]
    t = b * seq
    tm = min(512, seq)
    blk = min(256, seq)

    km_all, vm_all = _mem_kv(mem.reshape(b * mem_len, d), g_mem, w_mem_kv.astype(BF16))
    mw = km_all.shape[-1]
    km_all = km_all.reshape(depth, b, mem_len, mw)
    vm_all = vm_all.reshape(depth, b, mem_len, mw)

    x2d = x.reshape(t, d)
    for i in range(depth):
        kind = i % N_MIXERS
        j = i // N_MIXERS
        w = w_in[i].astype(BF16)
        if kind == 2:
            mix, mq = _conv_proj(x2d, g_mix[i], w, conv_w[j], seq=seq, tm=tm)
        else:
            q, k, v, mq = _in_proj(x2d, g_mix[i], w, q_scale=HEAD_DIM ** -0.5, tm=tm)
            q, k, v = (a.reshape(b, seq, d) for a in (q, k, v))
            if kind == 0:
                mix = _sb_attention(q, k, v, blk=blk)
            else:
                mix = _diff_attention(q, k, v, lam_q1[j], lam_k1[j], lam_q2[j], lam_k2[j],
                                      g_diff_head[j], layer_idx=i, blk=blk)
            mix = mix.reshape(t, d)
        x2d = _post(mix, mq, x2d, km_all[i], vm_all[i], w_o[i].astype(BF16), g_ffn[i],
                    w_ffn_in[i].astype(BF16), w_ffn_out[i].astype(BF16), g_final,
                    seq=seq, tm=tm, final_norm=(i == depth - 1))
    return x2d.reshape(b, seq, d)
```

```python
import functools
import math

import jax
import jax.numpy as jnp
from jax import lax
from jax.experimental import pallas as pl
from jax.experimental.pallas import tpu as pltpu

F32 = jnp.float32
BF16 = jnp.bfloat16

N_MIXERS = 3
HEAD_DIM = 128
DIFF_V_DIM = 2 * HEAD_DIM
MEM_HEADS = 4
CONV_WIDTH = 3
NORM_EPS = 1e-6
HEAD_NORM_EPS = 1e-5
MASK_VALUE = -1e30
SOFTPLUS_LINEAR_ABOVE = 20.0

V7X_VMEM_BYTES = 64 * 1024 * 1024
V7X_MXU_DIM = 256
LOG2_E = 1.4426950408889634
SUBLANES = 8


def _vmem_limit(estimate_bytes):
    return int(min(max(estimate_bytes, 16 * 1024 * 1024), V7X_VMEM_BYTES - 8 * 1024 * 1024))


def _rms(xf, g, eps):
    ms = jnp.mean(xf * xf, axis=-1, keepdims=True)
    return xf * lax.rsqrt(ms + eps) * g


def _dot(a, b):
    return jnp.dot(a, b, preferred_element_type=F32)


def _dot_nt(a, b):
    return lax.dot_general(a, b, (((1,), (1,)), ((), ())), preferred_element_type=F32)


def _const_spec(shape):
    nd = len(shape)
    return pl.BlockSpec(shape, lambda *_: (0,) * nd, pipeline_mode=pl.Buffered(1))


def _mem_kv_kernel(mem_ref, g_ref, w_ref, k_ref, v_ref, *, mem_width):
    mem_n = _rms(mem_ref[...], g_ref[...], NORM_EPS).astype(BF16)
    kv = _dot(mem_n, w_ref[0])
    head_dim = mem_width // MEM_HEADS
    k_ref[0] = (kv[:, :mem_width] * head_dim ** -0.5).astype(BF16)
    v_ref[0] = kv[:, mem_width:].astype(BF16)


def _mem_kv(mem2d, g_mem, w_kv):
    depth, d_model, two_w = w_kv.shape
    mem_width = two_w // 2
    rows = mem2d.shape[0]
    return pl.pallas_call(
        functools.partial(_mem_kv_kernel, mem_width=mem_width),
        grid=(depth,),
        in_specs=[
            pl.BlockSpec((rows, d_model), lambda i: (0, 0)),
            pl.BlockSpec((1, d_model), lambda i: (0, 0)),
            pl.BlockSpec((1, d_model, two_w), lambda i: (i, 0, 0)),
        ],
        out_specs=[
            pl.BlockSpec((1, rows, mem_width), lambda i: (i, 0, 0)),
            pl.BlockSpec((1, rows, mem_width), lambda i: (i, 0, 0)),
        ],
        out_shape=[jax.ShapeDtypeStruct((depth, rows, mem_width), BF16)] * 2,
        name="mem_kv",
    )(mem2d, g_mem.reshape(1, d_model), w_kv)


def _in_proj_kernel(x_ref, g_ref, w_ref, q_ref, k_ref, v_ref, mq_ref, *, q_scale):
    d = x_ref.shape[1]
    hn = _rms(x_ref[...], g_ref[...], NORM_EPS).astype(BF16)
    q_ref[...] = (_dot(hn, w_ref[:, 0:d]) * q_scale).astype(BF16)
    k_ref[...] = _dot(hn, w_ref[:, d:2 * d]).astype(BF16)
    v_ref[...] = _dot(hn, w_ref[:, 2 * d:3 * d]).astype(BF16)
    mq_ref[...] = _dot(hn, w_ref[:, 3 * d:]).astype(BF16)


def _in_proj(x2d, g, w, *, q_scale, tm):
    t, d = x2d.shape
    n = w.shape[1]
    mw = n - 3 * d
    est = 2 * tm * d * 4 + d * n * 2 + 2 * tm * n * 2 + 3 * tm * d * 4
    return pl.pallas_call(
        functools.partial(_in_proj_kernel, q_scale=q_scale),
        grid=(t // tm,),
        in_specs=[
            pl.BlockSpec((tm, d), lambda i: (i, 0)),
            _const_spec((1, d)),
            _const_spec((d, n)),
        ],
        out_specs=[
            pl.BlockSpec((tm, d), lambda i: (i, 0)),
            pl.BlockSpec((tm, d), lambda i: (i, 0)),
            pl.BlockSpec((tm, d), lambda i: (i, 0)),
            pl.BlockSpec((tm, mw), lambda i: (i, 0)),
        ],
        out_shape=[jax.ShapeDtypeStruct((t, d), BF16)] * 3
        + [jax.ShapeDtypeStruct((t, mw), BF16)],
        compiler_params=pltpu.CompilerParams(
            dimension_semantics=("arbitrary",), vmem_limit_bytes=_vmem_limit(est)),
        name="in_proj",
    )(x2d, g.reshape(1, d), w)


def _conv_proj_kernel(x_ref, g_ref, w_ref, cw_ref, mix_ref, mq_ref, u_ref, *, tiles_per_seq):
    tm, d = x_ref.shape
    i = pl.program_id(0)

    @pl.when(i % tiles_per_seq == 0)
    def _():
        u_ref[0:SUBLANES, :] = jnp.zeros((SUBLANES, d), F32)

    hn = _rms(x_ref[...], g_ref[...], NORM_EPS).astype(BF16)
    gate_c = _dot(hn, w_ref[:, d:2 * d])
    h = _dot(hn, w_ref[:, 2 * d:3 * d])
    u_ref[SUBLANES:SUBLANES + tm, :] = gate_c * h
    y = cw_ref[CONV_WIDTH - 1:CONV_WIDTH, :] * u_ref[SUBLANES:SUBLANES + tm, :]
    for tap in range(CONV_WIDTH - 1):
        shift = CONV_WIDTH - 1 - tap
        y = y + cw_ref[tap:tap + 1, :] * u_ref[SUBLANES - shift:SUBLANES - shift + tm, :]
    gate_b = _dot(hn, w_ref[:, 0:d])
    mix_ref[...] = (gate_b * y).astype(BF16)
    mq_ref[...] = _dot(hn, w_ref[:, 3 * d:]).astype(BF16)
    u_ref[0:SUBLANES, :] = u_ref[tm:tm + SUBLANES, :]


def _conv_proj(x2d, g, w, conv_w, *, seq, tm):
    t, d = x2d.shape
    n = w.shape[1]
    mw = n - 3 * d
    est = 2 * tm * d * 4 + d * n * 2 + 2 * tm * (d + mw) * 2 + 5 * tm * d * 4
    return pl.pallas_call(
        functools.partial(_conv_proj_kernel, tiles_per_seq=seq // tm),
        grid=(t // tm,),
        in_specs=[
            pl.BlockSpec((tm, d), lambda i: (i, 0)),
            _const_spec((1, d)),
            _const_spec((d, n)),
            _const_spec((CONV_WIDTH, d)),
        ],
        out_specs=[
            pl.BlockSpec((tm, d), lambda i: (i, 0)),
            pl.BlockSpec((tm, mw), lambda i: (i, 0)),
        ],
        out_shape=[jax.ShapeDtypeStruct((t, d), BF16), jax.ShapeDtypeStruct((t, mw), BF16)],
        scratch_shapes=[pltpu.VMEM((tm + SUBLANES, d), F32)],
        compiler_params=pltpu.CompilerParams(
            dimension_semantics=("arbitrary",), vmem_limit_bytes=_vmem_limit(est)),
        name="conv_proj",
    )(x2d, g.reshape(1, d), w, conv_w)


def _sb_kernel(q_ref, k_ref, v_ref, tri_ref, o_ref, *, blk, sub):
    i = pl.program_id(2)
    q = q_ref[0]
    row = lax.broadcasted_iota(jnp.int32, (blk, blk), 0)
    col = lax.broadcasted_iota(jnp.int32, (blk, blk), 1)
    below_diag = col < row

    def block(j, acc, run, mask):
        start = pl.multiple_of(j * blk, blk)
        ks = k_ref[0, pl.ds(start, blk), :]
        vs = v_ref[0, pl.ds(start, blk), :]
        z2 = _dot_nt(q, ks) * LOG2_E
        u = jnp.where(z2 > SOFTPLUS_LINEAR_ABOVE * LOG2_E, z2, jnp.log2(1.0 + jnp.exp2(z2)))
        if mask is not None:
            u = jnp.where(mask, u, 0.0)
        u_hi = u.astype(BF16)
        u_lo = (u - u_hi.astype(F32)).astype(BF16)
        log_a = [None] * (blk // sub)
        for sb in reversed(range(blk // sub)):
            sl = slice(sb * sub, (sb + 1) * sub)
            suffix = _dot(jnp.concatenate([u_hi[:, sl], u_lo[:, sl]], axis=1), tri_ref[...])
            log_a[sb] = z2[:, sl] - suffix - run
            run = run + jnp.sum(u[:, sl], axis=1, keepdims=True)
        a = jnp.exp2(jnp.concatenate(log_a, axis=1))
        if mask is not None:
            a = jnp.where(mask, a, 0.0)
        acc = acc + _dot(a.astype(BF16), vs)
        return acc, run

    acc0 = jnp.zeros((blk, q.shape[1]), F32)
    run0 = jnp.zeros((blk, 1), F32)
    acc, run = block(i, acc0, run0, below_diag)

    def body(jj, carry):
        return block(i - 1 - jj, carry[0], carry[1], None)

    acc, run = lax.fori_loop(0, i, body, (acc, run))
    o_ref[0] = acc.astype(o_ref.dtype)


def _sb_attention(q, k, v, *, blk):
    b, s, d = q.shape
    heads = d // HEAD_DIM
    sub = min(blk, V7X_MXU_DIM)
    r = jnp.arange(sub)
    tri = (r[:, None] >= r[None, :]).astype(BF16)
    tri2 = jnp.concatenate([tri, tri], axis=0)
    est = 2 * s * HEAD_DIM * 2 + 8 * blk * blk * 4
    return pl.pallas_call(
        functools.partial(_sb_kernel, blk=blk, sub=sub),
        grid=(b, heads, s // blk),
        in_specs=[
            pl.BlockSpec((1, blk, HEAD_DIM), lambda bi, h, i: (bi, i, h)),
            pl.BlockSpec((1, s, HEAD_DIM), lambda bi, h, i: (bi, 0, h), pipeline_mode=pl.Buffered(1)),
            pl.BlockSpec((1, s, HEAD_DIM), lambda bi, h, i: (bi, 0, h), pipeline_mode=pl.Buffered(1)),
            _const_spec((2 * sub, sub)),
        ],
        out_specs=pl.BlockSpec((1, blk, HEAD_DIM), lambda bi, h, i: (bi, i, h)),
        out_shape=jax.ShapeDtypeStruct((b, s, d), BF16),
        compiler_params=pltpu.CompilerParams(
            dimension_semantics=("arbitrary", "arbitrary", "arbitrary"),
            vmem_limit_bytes=_vmem_limit(est)),
        name="sb_attention",
    )(q, k, v, tri2)


def _diff_kernel(slopes_ref, lq1_ref, lk1_ref, lq2_ref, lk2_ref, g_ref,
                 q1_ref, q2_ref, k1_ref, k2_ref, v_ref, o_ref, *, blk, lambda_init):
    h = pl.program_id(1)
    i = pl.program_id(2)
    slope = slopes_ref[h]
    q1 = q1_ref[0]
    q2 = q2_ref[0]
    row = lax.broadcasted_iota(jnp.int32, (blk, blk), 0)
    col = lax.broadcasted_iota(jnp.int32, (blk, blk), 1)
    causal = col <= row
    slope2 = slope * LOG2_E
    local_bias = (col - row).astype(F32) * slope2

    def one_map(qx, ks, vs, bias, mask, m, l, acc):
        s = _dot_nt(qx, ks) * LOG2_E + bias
        if mask is not None:
            s = jnp.where(mask, s, MASK_VALUE)
        m_new = jnp.maximum(m, jnp.max(s, axis=1, keepdims=True))
        alpha = jnp.exp2(m - m_new)
        p = jnp.exp2(s - m_new)
        l = alpha * l + jnp.sum(p, axis=1, keepdims=True)
        acc = alpha * acc + _dot(p.astype(BF16), vs)
        return m_new, l, acc

    def block(j, carry, mask):
        start = pl.multiple_of(j * blk, blk)
        vs = v_ref[0, pl.ds(start, blk), :]
        bias = local_bias - slope2 * ((i - j) * blk).astype(F32)
        st1 = one_map(q1, k1_ref[0, pl.ds(start, blk), :], vs, bias, mask, *carry[0:3])
        st2 = one_map(q2, k2_ref[0, pl.ds(start, blk), :], vs, bias, mask, *carry[3:6])
        return st1 + st2

    m0 = jnp.full((blk, 1), MASK_VALUE, F32)
    l0 = jnp.zeros((blk, 1), F32)
    a0 = jnp.zeros((blk, v_ref.shape[2]), F32)
    carry = block(i, (m0, l0, a0, m0, l0, a0), causal)
    carry = lax.fori_loop(0, i, lambda jj, cr: block(i - 1 - jj, cr, None), carry)
    _, l1, acc1, _, l2, acc2 = carry

    lam = (jnp.exp(jnp.sum(lq1_ref[...] * lk1_ref[...], axis=1, keepdims=True))
           - jnp.exp(jnp.sum(lq2_ref[...] * lk2_ref[...], axis=1, keepdims=True))
           + lambda_init)
    o = acc1 / l1 - lam * (acc2 / l2)
    o = _rms(o, g_ref[...], HEAD_NORM_EPS) * (1.0 - lambda_init)
    o_ref[0] = o.astype(o_ref.dtype)


def _diff_attention(q, k, v, lq1, lk1, lq2, lk2, g_head, *, layer_idx, blk):
    b, s, d = q.shape
    heads = d // DIFF_V_DIM
    lambda_init = 0.8 - 0.6 * math.exp(-0.3 * layer_idx)
    slopes = 2.0 ** (-8.0 * jnp.arange(1, heads + 1, dtype=F32) / heads)
    vec = lambda a: a.reshape(1, -1).astype(F32)
    est = s * (2 * HEAD_DIM + DIFF_V_DIM) * 2 + 12 * blk * blk * 4 + 4 * blk * DIFF_V_DIM * 4
    return pl.pallas_call(
        functools.partial(_diff_kernel, blk=blk, lambda_init=lambda_init),
        grid=(b, heads, s // blk),
        in_specs=[
            pl.BlockSpec(memory_space=pltpu.SMEM),
            _const_spec((1, HEAD_DIM)), _const_spec((1, HEAD_DIM)),
            _const_spec((1, HEAD_DIM)), _const_spec((1, HEAD_DIM)),
            _const_spec((1, DIFF_V_DIM)),
            pl.BlockSpec((1, blk, HEAD_DIM), lambda bi, h, i: (bi, i, 2 * h)),
            pl.BlockSpec((1, blk, HEAD_DIM), lambda bi, h, i: (bi, i, 2 * h + 1)),
            pl.BlockSpec((1, s, HEAD_DIM), lambda bi, h, i: (bi, 0, 2 * h),
                         pipeline_mode=pl.Buffered(1)),
            pl.BlockSpec((1, s, HEAD_DIM), lambda bi, h, i: (bi, 0, 2 * h + 1),
                         pipeline_mode=pl.Buffered(1)),
            pl.BlockSpec((1, s, DIFF_V_DIM), lambda bi, h, i: (bi, 0, h),
                         pipeline_mode=pl.Buffered(1)),
        ],
        out_specs=pl.BlockSpec((1, blk, DIFF_V_DIM), lambda bi, h, i: (bi, i, h)),
        out_shape=jax.ShapeDtypeStruct((b, s, d), BF16),
        compiler_params=pltpu.CompilerParams(
            dimension_semantics=("arbitrary", "arbitrary", "arbitrary"),
            vmem_limit_bytes=_vmem_limit(est)),
        name="diff_attention",
    )(slopes, vec(lq1), vec(lk1), vec(lq2), vec(lk2), vec(g_head), q, q, k, k, v)


def _post_kernel(mix_ref, mq_ref, x_ref, km_ref, vm_ref, wo_ref, g_ref, w1_ref, w2_ref, gf_ref,
                 o_ref, *, ff_chunks, final_norm):
    tm, d = x_ref.shape
    mw = mq_ref.shape[1]
    d_ff = w2_ref.shape[0]
    head_dim = mw // MEM_HEADS

    mq = mq_ref[...]
    km = km_ref[0]
    vm = vm_ref[0]
    lane_head = lax.broadcasted_iota(jnp.int32, (tm, mw), 1) // head_dim
    mo = jnp.zeros((tm, mw), F32)
    for hd in range(MEM_HEADS):
        in_head = lane_head == hd
        s = _dot_nt(jnp.where(in_head, mq, jnp.zeros_like(mq)), km)
        p = jnp.exp(s - jnp.max(s, axis=1, keepdims=True))
        l = jnp.sum(p, axis=1, keepdims=True)
        mo = jnp.where(in_head, _dot(p.astype(BF16), vm) / l, mo)

    y = x_ref[...] + _dot(mix_ref[...], wo_ref[0:d, :]) + _dot(mo.astype(BF16), wo_ref[d:, :])

    yn = _rms(y, g_ref[...], NORM_EPS).astype(BF16)
    hidden = []
    for lo, hi in ff_chunks:
        gate = _dot(yn, w1_ref[:, lo:hi])
        up = _dot(yn, w1_ref[:, d_ff + lo:d_ff + hi])
        hidden.append((gate * (1.0 / (1.0 + jnp.exp(-gate))) * up).astype(BF16))
    y = y + _dot(jnp.concatenate(hidden, axis=1), w2_ref[...])
    if final_norm:
        y = _rms(y, gf_ref[...], NORM_EPS)
    o_ref[...] = y


def _ff_chunks(d_ff, n_chunks, align):
    tiles = d_ff // align
    bounds = [align * ((tiles * c) // n_chunks) for c in range(n_chunks)] + [d_ff]
    return tuple((bounds[c], bounds[c + 1]) for c in range(n_chunks))


def _post(mix2d, mq2d, x2d, km, vm, wo, g_ffn, w1, w2, g_final, *, seq, tm, final_norm):
    t, d = x2d.shape
    mw = mq2d.shape[1]
    mem_len = km.shape[1]
    d_ff = w2.shape[0]
    tiles_per_seq = seq // tm
    ff_chunks = _ff_chunks(d_ff, 2, 256)
    chunk = max(hi - lo for lo, hi in ff_chunks)
    est = ((wo.size + w1.size + w2.size) * 2 + 2 * tm * (d + mw) * 2 + 4 * tm * d * 4
           + tm * chunk * 12 + 4 * tm * d * 4)
    return pl.pallas_call(
        functools.partial(_post_kernel, ff_chunks=ff_chunks, final_norm=final_norm),
        grid=(t // tm,),
        in_specs=[
            pl.BlockSpec((tm, d), lambda i: (i, 0)),
            pl.BlockSpec((tm, mw), lambda i: (i, 0)),
            pl.BlockSpec((tm, d), lambda i: (i, 0)),
            pl.BlockSpec((1, mem_len, mw), lambda i: (i // tiles_per_seq, 0, 0)),
            pl.BlockSpec((1, mem_len, mw), lambda i: (i // tiles_per_seq, 0, 0)),
            _const_spec(wo.shape),
            _const_spec((1, d)),
            _const_spec(w1.shape),
            _const_spec(w2.shape),
            _const_spec((1, d)),
        ],
        out_specs=pl.BlockSpec((tm, d), lambda i: (i, 0)),
        out_shape=jax.ShapeDtypeStruct((t, d), F32),
        compiler_params=pltpu.CompilerParams(
            dimension_semantics=("arbitrary",), vmem_limit_bytes=_vmem_limit(est)),
        name="post",
    )(mix2d, mq2d, x2d, km, vm, wo, g_ffn.reshape(1, d), w1, w2, g_final.reshape(1, d))


def kernel(x, mem, g_mix, w_in, w_mem_kv, w_o, g_ffn, w_ffn_in, w_ffn_out,
           lam_q1, lam_k1, lam_q2, lam_k2, g_diff_head, conv_w, g_mem, g_final):
    b, seq, d = x.shape
    mem_len = mem.shape[1]
    depth = w_in.shape[0]
    t = b * seq
    tm = min(512, seq)
    sb_blk = min(1024, seq)
    diff_blk = min(512, seq)

    km_all, vm_all = _mem_kv(mem.reshape(b * mem_len, d), g_mem, w_mem_kv.astype(BF16))
    mw = km_all.shape[-1]
    km_all = km_all.reshape(depth, b, mem_len, mw)
    vm_all = vm_all.reshape(depth, b, mem_len, mw)

    x2d = x.reshape(t, d)
    for i in range(depth):
        kind = i % N_MIXERS
        j = i // N_MIXERS
        w = w_in[i].astype(BF16)
        if kind == 2:
            mix, mq = _conv_proj(x2d, g_mix[i], w, conv_w[j], seq=seq, tm=tm)
        else:
            q, k, v, mq = _in_proj(x2d, g_mix[i], w, q_scale=HEAD_DIM ** -0.5, tm=tm)
            q, k, v = (a.reshape(b, seq, d) for a in (q, k, v))
            if kind == 0:
                mix = _sb_attention(q, k, v, blk=sb_blk)
            else:
                mix = _diff_attention(q, k, v, lam_q1[j], lam_k1[j], lam_q2[j], lam_k2[j],
                                      g_diff_head[j], layer_idx=i, blk=diff_blk)
            mix = mix.reshape(t, d)
        x2d = _post(mix, mq, x2d, km_all[i], vm_all[i], w_o[i].astype(BF16), g_ffn[i],
                    w_ffn_in[i].astype(BF16), w_ffn_out[i].astype(BF16), g_final,
                    seq=seq, tm=tm, final_norm=(i == depth - 1))
    return x2d.reshape(b, seq, d)
```

```python
import functools
import math

import jax
import jax.numpy as jnp
from jax import lax
from jax.experimental import pallas as pl
from jax.experimental.pallas import tpu as pltpu

F32 = jnp.float32
BF16 = jnp.bfloat16

N_MIXERS = 3
HEAD_DIM = 128
DIFF_V_DIM = 2 * HEAD_DIM
MEM_HEADS = 4
CONV_WIDTH = 3
NORM_EPS = 1e-6
HEAD_NORM_EPS = 1e-5
MASK_VALUE = -1e30
EXP2_ARG_MAX = 126.0
ALIBI_SLOPE_PIECES = 3
ALIBI_POS_RADIX = 16

V7X_VMEM_BYTES = 64 * 1024 * 1024
V7X_MXU_DIM = 256
LOG2_E = 1.4426950408889634
SUBLANES = 8


def _vmem_limit(estimate_bytes):
    return int(min(max(estimate_bytes, 16 * 1024 * 1024), V7X_VMEM_BYTES - 8 * 1024 * 1024))


def _rms(xf, g, eps):
    ms = jnp.mean(xf * xf, axis=-1, keepdims=True)
    return xf * lax.rsqrt(ms + eps) * g


def _dot(a, b):
    return jnp.dot(a, b, preferred_element_type=F32)


def _dot_nt(a, b):
    return lax.dot_general(a, b, (((1,), (1,)), ((), ())), preferred_element_type=F32)


def _const_spec(shape):
    nd = len(shape)
    return pl.BlockSpec(shape, lambda *_: (0,) * nd, pipeline_mode=pl.Buffered(1))


def _mem_kv_kernel(mem_ref, g_ref, w_ref, k_ref, v_ref, *, mem_width):
    mem_n = _rms(mem_ref[...], g_ref[...], NORM_EPS).astype(BF16)
    kv = _dot(mem_n, w_ref[0])
    head_dim = mem_width // MEM_HEADS
    k_ref[0] = (kv[:, :mem_width] * head_dim ** -0.5).astype(BF16)
    v_ref[0] = kv[:, mem_width:].astype(BF16)


def _mem_kv(mem2d, g_mem, w_kv):
    depth, d_model, two_w = w_kv.shape
    mem_width = two_w // 2
    rows = mem2d.shape[0]
    return pl.pallas_call(
        functools.partial(_mem_kv_kernel, mem_width=mem_width),
        grid=(depth,),
        in_specs=[
            pl.BlockSpec((rows, d_model), lambda i: (0, 0)),
            pl.BlockSpec((1, d_model), lambda i: (0, 0)),
            pl.BlockSpec((1, d_model, two_w), lambda i: (i, 0, 0)),
        ],
        out_specs=[
            pl.BlockSpec((1, rows, mem_width), lambda i: (i, 0, 0)),
            pl.BlockSpec((1, rows, mem_width), lambda i: (i, 0, 0)),
        ],
        out_shape=[jax.ShapeDtypeStruct((depth, rows, mem_width), BF16)] * 2,
        name="mem_kv",
    )(mem2d, g_mem.reshape(1, d_model), w_kv)


def _in_proj_kernel(x_ref, g_ref, w_ref, q_ref, k_ref, v_ref, mq_ref, *, q_scale):
    d = x_ref.shape[1]
    hn = _rms(x_ref[...], g_ref[...], NORM_EPS).astype(BF16)
    q_ref[...] = (_dot(hn, w_ref[:, 0:d]) * q_scale).astype(BF16)
    k_ref[...] = _dot(hn, w_ref[:, d:2 * d]).astype(BF16)
    v_ref[...] = _dot(hn, w_ref[:, 2 * d:3 * d]).astype(BF16)
    mq_ref[...] = _dot(hn, w_ref[:, 3 * d:]).astype(BF16)


def _in_proj(x2d, g, w, *, q_scale, tm):
    t, d = x2d.shape
    n = w.shape[1]
    mw = n - 3 * d
    est = 2 * tm * d * 4 + d * n * 2 + 2 * tm * n * 2 + 3 * tm * d * 4
    return pl.pallas_call(
        functools.partial(_in_proj_kernel, q_scale=q_scale),
        grid=(t // tm,),
        in_specs=[
            pl.BlockSpec((tm, d), lambda i: (i, 0)),
            _const_spec((1, d)),
            _const_spec((d, n)),
        ],
        out_specs=[
            pl.BlockSpec((tm, d), lambda i: (i, 0)),
            pl.BlockSpec((tm, d), lambda i: (i, 0)),
            pl.BlockSpec((tm, d), lambda i: (i, 0)),
            pl.BlockSpec((tm, mw), lambda i: (i, 0)),
        ],
        out_shape=[jax.ShapeDtypeStruct((t, d), BF16)] * 3
        + [jax.ShapeDtypeStruct((t, mw), BF16)],
        compiler_params=pltpu.CompilerParams(
            dimension_semantics=("arbitrary",), vmem_limit_bytes=_vmem_limit(est)),
        name="in_proj",
    )(x2d, g.reshape(1, d), w)


def _conv_proj_kernel(x_ref, g_ref, w_ref, cw_ref, mix_ref, mq_ref, u_ref, *, tiles_per_seq):
    tm, d = x_ref.shape
    i = pl.program_id(0)

    @pl.when(i % tiles_per_seq == 0)
    def _():
        u_ref[0:SUBLANES, :] = jnp.zeros((SUBLANES, d), F32)

    hn = _rms(x_ref[...], g_ref[...], NORM_EPS).astype(BF16)
    gate_c = _dot(hn, w_ref[:, d:2 * d])
    h = _dot(hn, w_ref[:, 2 * d:3 * d])
    u_ref[SUBLANES:SUBLANES + tm, :] = gate_c * h
    y = cw_ref[CONV_WIDTH - 1:CONV_WIDTH, :] * u_ref[SUBLANES:SUBLANES + tm, :]
    for tap in range(CONV_WIDTH - 1):
        shift = CONV_WIDTH - 1 - tap
        y = y + cw_ref[tap:tap + 1, :] * u_ref[SUBLANES - shift:SUBLANES - shift + tm, :]
    gate_b = _dot(hn, w_ref[:, 0:d])
    mix_ref[...] = (gate_b * y).astype(BF16)
    mq_ref[...] = _dot(hn, w_ref[:, 3 * d:]).astype(BF16)
    u_ref[0:SUBLANES, :] = u_ref[tm:tm + SUBLANES, :]


def _conv_proj(x2d, g, w, conv_w, *, seq, tm):
    t, d = x2d.shape
    n = w.shape[1]
    mw = n - 3 * d
    est = 2 * tm * d * 4 + d * n * 2 + 2 * tm * (d + mw) * 2 + 5 * tm * d * 4
    return pl.pallas_call(
        functools.partial(_conv_proj_kernel, tiles_per_seq=seq // tm),
        grid=(t // tm,),
        in_specs=[
            pl.BlockSpec((tm, d), lambda i: (i, 0)),
            _const_spec((1, d)),
            _const_spec((d, n)),
            _const_spec((CONV_WIDTH, d)),
        ],
        out_specs=[
            pl.BlockSpec((tm, d), lambda i: (i, 0)),
            pl.BlockSpec((tm, mw), lambda i: (i, 0)),
        ],
        out_shape=[jax.ShapeDtypeStruct((t, d), BF16), jax.ShapeDtypeStruct((t, mw), BF16)],
        scratch_shapes=[pltpu.VMEM((tm + SUBLANES, d), F32)],
        compiler_params=pltpu.CompilerParams(
            dimension_semantics=("arbitrary",), vmem_limit_bytes=_vmem_limit(est)),
        name="conv_proj",
    )(x2d, g.reshape(1, d), w, conv_w)


def _sb_kernel(q_ref, k_ref, v_ref, tri_ref, o_ref, *, blk, sub):
    i = pl.program_id(2)
    q = q_ref[0]
    row = lax.broadcasted_iota(jnp.int32, (blk, blk), 0)
    col = lax.broadcasted_iota(jnp.int32, (blk, blk), 1)
    below_diag = col < row

    def block(j, acc, run, mask):
        start = pl.multiple_of(j * blk, blk)
        ks = k_ref[0, pl.ds(start, blk), :]
        vs = v_ref[0, pl.ds(start, blk), :]
        z2 = jnp.minimum(_dot_nt(q, ks), EXP2_ARG_MAX)
        u = jnp.log2(1.0 + jnp.exp2(z2))
        if mask is not None:
            u = jnp.where(mask, u, 0.0)
        u_bf = u.astype(BF16)
        log_a = [None] * (blk // sub)
        for sb in reversed(range(blk // sub)):
            sl = slice(sb * sub, (sb + 1) * sub)
            suffix = _dot(u_bf[:, sl], tri_ref[...])
            log_a[sb] = z2[:, sl] - suffix - run
            run = run + jnp.sum(u[:, sl], axis=1, keepdims=True)
        a = jnp.exp2(jnp.concatenate(log_a, axis=1))
        if mask is not None:
            a = jnp.where(mask, a, 0.0)
        acc = acc + _dot(a.astype(BF16), vs)
        return acc, run

    acc0 = jnp.zeros((blk, q.shape[1]), F32)
    run0 = jnp.zeros((blk, 1), F32)
    acc, run = block(i, acc0, run0, below_diag)

    def body(jj, carry):
        return block(i - 1 - jj, carry[0], carry[1], None)

    acc, run = lax.fori_loop(0, i, body, (acc, run))
    o_ref[0] = acc.astype(o_ref.dtype)


def _sb_attention(q, k, v, *, blk):
    b, s, d = q.shape
    heads = d // HEAD_DIM
    sub = min(blk, V7X_MXU_DIM)
    r = jnp.arange(sub)
    tri = (r[:, None] >= r[None, :]).astype(BF16)
    est = 2 * s * HEAD_DIM * 2 + 8 * blk * blk * 4
    return pl.pallas_call(
        functools.partial(_sb_kernel, blk=blk, sub=sub),
        grid=(b, heads, s // blk),
        in_specs=[
            pl.BlockSpec((1, blk, HEAD_DIM), lambda bi, h, i: (bi, i, h)),
            pl.BlockSpec((1, s, HEAD_DIM), lambda bi, h, i: (bi, 0, h), pipeline_mode=pl.Buffered(1)),
            pl.BlockSpec((1, s, HEAD_DIM), lambda bi, h, i: (bi, 0, h), pipeline_mode=pl.Buffered(1)),
            _const_spec((sub, sub)),
        ],
        out_specs=pl.BlockSpec((1, blk, HEAD_DIM), lambda bi, h, i: (bi, i, h)),
        out_shape=jax.ShapeDtypeStruct((b, s, d), BF16),
        compiler_params=pltpu.CompilerParams(
            dimension_semantics=("arbitrary", "arbitrary", "arbitrary"),
            vmem_limit_bytes=_vmem_limit(est)),
        name="sb_attention",
    )(q, k, v, tri)


def _diff_kernel(slope2_ref, lq1_ref, lk1_ref, lq2_ref, lk2_ref, g_ref, qf_ref, kf_ref,
                 q1_ref, q2_ref, k1_ref, k2_ref, v_ref, o_ref, *, blk, lambda_init):
    h = pl.program_id(1)
    i = pl.program_id(2)
    slope2 = slope2_ref[h]
    qf = jnp.broadcast_to(qf_ref[0], (blk, qf_ref.shape[2]))
    q1 = jnp.concatenate([q1_ref[0], qf], axis=1)
    q2 = jnp.concatenate([q2_ref[0], qf], axis=1)
    kf = kf_ref[...]
    row = lax.broadcasted_iota(jnp.int32, (blk, blk), 0)
    col = lax.broadcasted_iota(jnp.int32, (blk, blk), 1)
    causal = col <= row

    def one_map(qx, ks, vs, off, mask, m, l, acc):
        s = _dot_nt(qx, jnp.concatenate([ks, kf], axis=1))
        if mask is not None:
            s = jnp.where(mask, s, MASK_VALUE)
        m_new = jnp.maximum(m, jnp.max(s, axis=1, keepdims=True) + off)
        alpha = jnp.exp2(m - m_new)
        p = jnp.exp2(s - (m_new - off))
        l = alpha * l + jnp.sum(p, axis=1, keepdims=True)
        acc = alpha * acc + _dot(p.astype(BF16), vs)
        return m_new, l, acc

    def block(j, carry, mask):
        start = pl.multiple_of(j * blk, blk)
        vs = v_ref[0, pl.ds(start, blk), :]
        off = -slope2 * ((i - j) * blk).astype(F32)
        st1 = one_map(q1, k1_ref[0, pl.ds(start, blk), :], vs, off, mask, *carry[0:3])
        st2 = one_map(q2, k2_ref[0, pl.ds(start, blk), :], vs, off, mask, *carry[3:6])
        return st1 + st2

    m0 = jnp.full((blk, 1), MASK_VALUE, F32)
    l0 = jnp.zeros((blk, 1), F32)
    a0 = jnp.zeros((blk, v_ref.shape[2]), F32)
    carry = block(i, (m0, l0, a0, m0, l0, a0), causal)
    carry = lax.fori_loop(0, i, lambda jj, cr: block(i - 1 - jj, cr, None), carry)
    _, l1, acc1, _, l2, acc2 = carry

    lam = (jnp.exp(jnp.sum(lq1_ref[...] * lk1_ref[...], axis=1, keepdims=True))
           - jnp.exp(jnp.sum(lq2_ref[...] * lk2_ref[...], axis=1, keepdims=True))
           + lambda_init)
    o = acc1 / l1 - lam * (acc2 / l2)
    o = _rms(o, g_ref[...], HEAD_NORM_EPS) * (1.0 - lambda_init)
    o_ref[0] = o.astype(o_ref.dtype)


def _alibi_features(slope2, blk):
    pieces = []
    rest = slope2
    for _ in range(ALIBI_SLOPE_PIECES):
        piece = rest.astype(BF16)
        pieces.append(piece)
        rest = rest - piece.astype(F32)
    qf = jnp.stack([p * r for p in pieces for r in (ALIBI_POS_RADIX, 1)], axis=1)
    pos = jnp.arange(blk)
    digits = jnp.stack([pos // ALIBI_POS_RADIX, pos % ALIBI_POS_RADIX] * ALIBI_SLOPE_PIECES, axis=1)
    pad = HEAD_DIM - 2 * ALIBI_SLOPE_PIECES
    qf = jnp.pad(qf.astype(BF16), ((0, 0), (0, pad)))[:, None, :]
    kf = jnp.pad(digits.astype(BF16), ((0, 0), (0, pad)))
    return qf, kf


def _diff_attention(q, k, v, lq1, lk1, lq2, lk2, g_head, *, layer_idx, blk):
    b, s, d = q.shape
    heads = d // DIFF_V_DIM
    assert blk <= ALIBI_POS_RADIX * 256
    lambda_init = 0.8 - 0.6 * math.exp(-0.3 * layer_idx)
    slope2 = LOG2_E * 2.0 ** (-8.0 * jnp.arange(1, heads + 1, dtype=F32) / heads)
    qf, kf = _alibi_features(slope2, blk)
    vec = lambda a: a.reshape(1, -1).astype(F32)
    est = s * (2 * HEAD_DIM + DIFF_V_DIM) * 2 + 12 * blk * blk * 4 + 4 * blk * DIFF_V_DIM * 4
    return pl.pallas_call(
        functools.partial(_diff_kernel, blk=blk, lambda_init=lambda_init),
        grid=(b, heads, s // blk),
        in_specs=[
            pl.BlockSpec(memory_space=pltpu.SMEM),
            _const_spec((1, HEAD_DIM)), _const_spec((1, HEAD_DIM)),
            _const_spec((1, HEAD_DIM)), _const_spec((1, HEAD_DIM)),
            _const_spec((1, DIFF_V_DIM)),
            pl.BlockSpec((1, 1, HEAD_DIM), lambda bi, h, i: (h, 0, 0)),
            _const_spec((blk, HEAD_DIM)),
            pl.BlockSpec((1, blk, HEAD_DIM), lambda bi, h, i: (bi, i, 2 * h)),
            pl.BlockSpec((1, blk, HEAD_DIM), lambda bi, h, i: (bi, i, 2 * h + 1)),
            pl.BlockSpec((1, s, HEAD_DIM), lambda bi, h, i: (bi, 0, 2 * h),
                         pipeline_mode=pl.Buffered(1)),
            pl.BlockSpec((1, s, HEAD_DIM), lambda bi, h, i: (bi, 0, 2 * h + 1),
                         pipeline_mode=pl.Buffered(1)),
            pl.BlockSpec((1, s, DIFF_V_DIM), lambda bi, h, i: (bi, 0, h),
                         pipeline_mode=pl.Buffered(1)),
        ],
        out_specs=pl.BlockSpec((1, blk, DIFF_V_DIM), lambda bi, h, i: (bi, i, h)),
        out_shape=jax.ShapeDtypeStruct((b, s, d), BF16),
        compiler_params=pltpu.CompilerParams(
            dimension_semantics=("arbitrary", "arbitrary", "arbitrary"),
            vmem_limit_bytes=_vmem_limit(est)),
        name="diff_attention",
    )(slope2, vec(lq1), vec(lk1), vec(lq2), vec(lk2), vec(g_head), qf, kf, q, q, k, k, v)


def _post_kernel(mix_ref, mq_ref, x_ref, km_ref, vm_ref, wo_ref, g_ref, w1_ref, w2_ref, gf_ref,
                 o_ref, *, ff_chunks, final_norm):
    tm, d = x_ref.shape
    mw = mq_ref.shape[1]
    d_ff = w2_ref.shape[0]
    head_dim = mw // MEM_HEADS

    mq = mq_ref[...]
    km = km_ref[0]
    vm = vm_ref[0]
    lane_head = lax.broadcasted_iota(jnp.int32, (tm, mw), 1) // head_dim
    mo = jnp.zeros((tm, mw), F32)
    for hd in range(MEM_HEADS):
        in_head = lane_head == hd
        s = _dot_nt(jnp.where(in_head, mq, jnp.zeros_like(mq)), km)
        p = jnp.exp(s - jnp.max(s, axis=1, keepdims=True))
        l = jnp.sum(p, axis=1, keepdims=True)
        mo = jnp.where(in_head, _dot(p.astype(BF16), vm) / l, mo)

    y = x_ref[...] + _dot(mix_ref[...], wo_ref[0:d, :]) + _dot(mo.astype(BF16), wo_ref[d:, :])

    yn = _rms(y, g_ref[...], NORM_EPS).astype(BF16)
    hidden = []
    for lo, hi in ff_chunks:
        gate = _dot(yn, w1_ref[:, lo:hi])
        up = _dot(yn, w1_ref[:, d_ff + lo:d_ff + hi])
        hidden.append((gate * (1.0 / (1.0 + jnp.exp(-gate))) * up).astype(BF16))
    y = y + _dot(jnp.concatenate(hidden, axis=1), w2_ref[...])
    if final_norm:
        y = _rms(y, gf_ref[...], NORM_EPS)
    o_ref[...] = y


def _ff_chunks(d_ff, n_chunks, align):
    tiles = d_ff // align
    bounds = [align * ((tiles * c) // n_chunks) for c in range(n_chunks)] + [d_ff]
    return tuple((bounds[c], bounds[c + 1]) for c in range(n_chunks))


def _post(mix2d, mq2d, x2d, km, vm, wo, g_ffn, w1, w2, g_final, *, seq, tm, final_norm):
    t, d = x2d.shape
    mw = mq2d.shape[1]
    mem_len = km.shape[1]
    d_ff = w2.shape[0]
    tiles_per_seq = seq // tm
    ff_chunks = _ff_chunks(d_ff, 2, 256)
    chunk = max(hi - lo for lo, hi in ff_chunks)
    est = ((wo.size + w1.size + w2.size) * 2 + 2 * tm * (d + mw) * 2 + 4 * tm * d * 4
           + tm * chunk * 12 + 4 * tm * d * 4)
    return pl.pallas_call(
        functools.partial(_post_kernel, ff_chunks=ff_chunks, final_norm=final_norm),
        grid=(t // tm,),
        in_specs=[
            pl.BlockSpec((tm, d), lambda i: (i, 0)),
            pl.BlockSpec((tm, mw), lambda i: (i, 0)),
            pl.BlockSpec((tm, d), lambda i: (i, 0)),
            pl.BlockSpec((1, mem_len, mw), lambda i: (i // tiles_per_seq, 0, 0)),
            pl.BlockSpec((1, mem_len, mw), lambda i: (i // tiles_per_seq, 0, 0)),
            _const_spec(wo.shape),
            _const_spec((1, d)),
            _const_spec(w1.shape),
            _const_spec(w2.shape),
            _const_spec((1, d)),
        ],
        out_specs=pl.BlockSpec((tm, d), lambda i: (i, 0)),
        out_shape=jax.ShapeDtypeStruct((t, d), F32),
        compiler_params=pltpu.CompilerParams(
            dimension_semantics=("arbitrary",), vmem_limit_bytes=_vmem_limit(est)),
        name="post",
    )(mix2d, mq2d, x2d, km, vm, wo, g_ffn.reshape(1, d), w1, w2, g_final.reshape(1, d))


def kernel(x, mem, g_mix, w_in, w_mem_kv, w_o, g_ffn, w_ffn_in, w_ffn_out,
           lam_q1, lam_k1, lam_q2, lam_k2, g_diff_head, conv_w, g_mem, g_final):
    b, seq, d = x.shape
    mem_len = mem.shape[1]
    depth = w_in.shape[0]
    t = b * seq
    tm = min(512, seq)
    sb_blk = min(1024, seq)
    diff_blk = min(512, seq)

    km_all, vm_all = _mem_kv(mem.reshape(b * mem_len, d), g_mem, w_mem_kv.astype(BF16))
    mw = km_all.shape[-1]
    km_all = km_all.reshape(depth, b, mem_len, mw)
    vm_all = vm_all.reshape(depth, b, mem_len, mw)

    x2d = x.reshape(t, d)
    for i in range(depth):
        kind = i % N_MIXERS
        j = i // N_MIXERS
        w = w_in[i].astype(BF16)
        if kind == 2:
            mix, mq = _conv_proj(x2d, g_mix[i], w, conv_w[j], seq=seq, tm=tm)
        else:
            q, k, v, mq = _in_proj(x2d, g_mix[i], w, q_scale=LOG2_E * HEAD_DIM ** -0.5, tm=tm)
            q, k, v = (a.reshape(b, seq, d) for a in (q, k, v))
            if kind == 0:
                mix = _sb_attention(q, k, v, blk=sb_blk)
            else:
                mix = _diff_attention(q, k, v, lam_q1[j], lam_k1[j], lam_q2[j], lam_k2[j],
                                      g_diff_head[j], layer_idx=i, blk=diff_blk)
            mix = mix.reshape(t, d)
        x2d = _post(mix, mq, x2d, km_all[i], vm_all[i], w_o[i].astype(BF16), g_ffn[i],
                    w_ffn_in[i].astype(BF16), w_ffn_out[i].astype(BF16), g_final,
                    seq=seq, tm=tm, final_norm=(i == depth - 1))
    return x2d.reshape(b, seq, d)
```

```python
import functools
import math

import jax
import jax.numpy as jnp
from jax import lax
from jax.experimental import pallas as pl
from jax.experimental.pallas import tpu as pltpu

F32 = jnp.float32
BF16 = jnp.bfloat16

N_MIXERS = 3
HEAD_DIM = 128
DIFF_V_DIM = 2 * HEAD_DIM
MEM_HEADS = 4
CONV_WIDTH = 3
NORM_EPS = 1e-6
HEAD_NORM_EPS = 1e-5
MASK_VALUE = -1e30
EXP2_ARG_MAX = 126.0
ALIBI_SLOPE_PIECES = 3
ALIBI_POS_RADIX = 16

V7X_VMEM_BYTES = 64 * 1024 * 1024
V7X_MXU_DIM = 256
V7X_MXUS = 2
LOG2_E = 1.4426950408889634
SUBLANES = 8
COL_REDUCE_WAYS = 8


def _vmem_limit(estimate_bytes):
    return int(min(max(estimate_bytes, 16 * 1024 * 1024), V7X_VMEM_BYTES - 8 * 1024 * 1024))


def _rms(xf, g, eps):
    ms = jnp.mean(xf * xf, axis=-1, keepdims=True)
    return xf * lax.rsqrt(ms + eps) * g


def _dot(a, b):
    return jnp.dot(a, b, preferred_element_type=F32)


def _dot_nt(a, b):
    return lax.dot_general(a, b, (((1,), (1,)), ((), ())), preferred_element_type=F32)


def _const_spec(shape):
    nd = len(shape)
    return pl.BlockSpec(shape, lambda *_: (0,) * nd, pipeline_mode=pl.Buffered(1))


def _mem_kv_kernel(mem_ref, g_ref, w_ref, k_ref, v_ref, *, mem_width):
    mem_n = _rms(mem_ref[...], g_ref[...], NORM_EPS).astype(BF16)
    kv = _dot(mem_n, w_ref[0])
    head_dim = mem_width // MEM_HEADS
    k_ref[0] = (kv[:, :mem_width] * head_dim ** -0.5).astype(BF16)
    v_ref[0] = kv[:, mem_width:].astype(BF16)


def _mem_kv(mem2d, g_mem, w_kv):
    depth, d_model, two_w = w_kv.shape
    mem_width = two_w // 2
    rows = mem2d.shape[0]
    return pl.pallas_call(
        functools.partial(_mem_kv_kernel, mem_width=mem_width),
        grid=(depth,),
        in_specs=[
            pl.BlockSpec((rows, d_model), lambda i: (0, 0)),
            pl.BlockSpec((1, d_model), lambda i: (0, 0)),
            pl.BlockSpec((1, d_model, two_w), lambda i: (i, 0, 0)),
        ],
        out_specs=[
            pl.BlockSpec((1, rows, mem_width), lambda i: (i, 0, 0)),
            pl.BlockSpec((1, rows, mem_width), lambda i: (i, 0, 0)),
        ],
        out_shape=[jax.ShapeDtypeStruct((depth, rows, mem_width), BF16)] * 2,
        name="mem_kv",
    )(mem2d, g_mem.reshape(1, d_model), w_kv)


def _in_proj_kernel(x_ref, g_ref, w_ref, q_ref, k_ref, vt_ref, mq_ref, *, q_scale):
    d = x_ref.shape[1]
    hn = _rms(x_ref[...], g_ref[...], NORM_EPS).astype(BF16)
    q_ref[...] = (_dot(hn, w_ref[:, 0:d]) * q_scale).astype(BF16)
    k_ref[...] = _dot(hn, w_ref[:, d:2 * d]).astype(BF16)
    vt_ref[0] = _dot(hn, w_ref[:, 2 * d:3 * d]).T.astype(BF16)
    mq_ref[...] = _dot(hn, w_ref[:, 3 * d:]).astype(BF16)


def _in_proj(x2d, g, w, *, q_scale, tm):
    t, d = x2d.shape
    n = w.shape[1]
    mw = n - 3 * d
    est = 2 * tm * d * 4 + d * n * 2 + 2 * tm * n * 2 + 3 * tm * d * 4
    return pl.pallas_call(
        functools.partial(_in_proj_kernel, q_scale=q_scale),
        grid=(t // tm,),
        in_specs=[
            pl.BlockSpec((tm, d), lambda i: (i, 0)),
            _const_spec((1, d)),
            _const_spec((d, n)),
        ],
        out_specs=[
            pl.BlockSpec((tm, d), lambda i: (i, 0)),
            pl.BlockSpec((tm, d), lambda i: (i, 0)),
            pl.BlockSpec((1, d, tm), lambda i: (i, 0, 0)),
            pl.BlockSpec((tm, mw), lambda i: (i, 0)),
        ],
        out_shape=[jax.ShapeDtypeStruct((t, d), BF16)] * 2
        + [jax.ShapeDtypeStruct((t // tm, d, tm), BF16), jax.ShapeDtypeStruct((t, mw), BF16)],
        compiler_params=pltpu.CompilerParams(
            dimension_semantics=("arbitrary",), vmem_limit_bytes=_vmem_limit(est)),
        name="in_proj",
    )(x2d, g.reshape(1, d), w)


def _conv_proj_kernel(x_ref, g_ref, w_ref, cw_ref, mix_ref, mq_ref, u_ref, *, tiles_per_seq):
    tm, d = x_ref.shape
    i = pl.program_id(0)

    @pl.when(i % tiles_per_seq == 0)
    def _():
        u_ref[0:SUBLANES, :] = jnp.zeros((SUBLANES, d), F32)

    hn = _rms(x_ref[...], g_ref[...], NORM_EPS).astype(BF16)
    gate_c = _dot(hn, w_ref[:, d:2 * d])
    h = _dot(hn, w_ref[:, 2 * d:3 * d])
    u_ref[SUBLANES:SUBLANES + tm, :] = gate_c * h
    y = cw_ref[CONV_WIDTH - 1:CONV_WIDTH, :] * u_ref[SUBLANES:SUBLANES + tm, :]
    for tap in range(CONV_WIDTH - 1):
        shift = CONV_WIDTH - 1 - tap
        y = y + cw_ref[tap:tap + 1, :] * u_ref[SUBLANES - shift:SUBLANES - shift + tm, :]
    gate_b = _dot(hn, w_ref[:, 0:d])
    mix_ref[...] = (gate_b * y).astype(BF16)
    mq_ref[...] = _dot(hn, w_ref[:, 3 * d:]).astype(BF16)
    u_ref[0:SUBLANES, :] = u_ref[tm:tm + SUBLANES, :]


def _conv_proj(x2d, g, w, conv_w, *, seq, tm):
    t, d = x2d.shape
    n = w.shape[1]
    mw = n - 3 * d
    est = 2 * tm * d * 4 + d * n * 2 + 2 * tm * (d + mw) * 2 + 5 * tm * d * 4
    return pl.pallas_call(
        functools.partial(_conv_proj_kernel, tiles_per_seq=seq // tm),
        grid=(t // tm,),
        in_specs=[
            pl.BlockSpec((tm, d), lambda i: (i, 0)),
            _const_spec((1, d)),
            _const_spec((d, n)),
            _const_spec((CONV_WIDTH, d)),
        ],
        out_specs=[
            pl.BlockSpec((tm, d), lambda i: (i, 0)),
            pl.BlockSpec((tm, mw), lambda i: (i, 0)),
        ],
        out_shape=[jax.ShapeDtypeStruct((t, d), BF16), jax.ShapeDtypeStruct((t, mw), BF16)],
        scratch_shapes=[pltpu.VMEM((tm + SUBLANES, d), F32)],
        compiler_params=pltpu.CompilerParams(
            dimension_semantics=("arbitrary",), vmem_limit_bytes=_vmem_limit(est)),
        name="conv_proj",
    )(x2d, g.reshape(1, d), w, conv_w)


def _col_reduce(x, op, reduce_fn):
    rows = x.shape[0]
    parts = [x[r * (rows // COL_REDUCE_WAYS):(r + 1) * (rows // COL_REDUCE_WAYS), :]
             for r in range(COL_REDUCE_WAYS)]
    while len(parts) > 1:
        parts = [op(parts[2 * r], parts[2 * r + 1]) for r in range(len(parts) // 2)]
    return reduce_fn(parts[0], axis=0, keepdims=True)


def _tile_validity(key_off, n_keys, qry_off, n_qry, strict):
    last_ok = key_off + n_keys - 1 < qry_off if strict else key_off + n_keys - 1 <= qry_off
    first_bad = key_off >= qry_off + n_qry - 1 if strict else key_off > qry_off + n_qry - 1
    return "all" if last_ok else ("none" if first_bad else "some")


def _sb_kernel(q_ref, k_ref, vt_ref, tri_ref, o_ref, run_ref, acc_ref, *, tq, tk, qc, sub):
    i = pl.program_id(2)
    run_ref[...] = jnp.zeros_like(run_ref)
    acc_ref[...] = jnp.zeros_like(acc_ref)

    def load_kv(jb):
        return k_ref[0, pl.ds(pl.multiple_of(jb * tk, tk), tk), :], vt_ref[0, jb]

    def scores(c, ks):
        return jnp.minimum(_dot_nt(ks, q_ref[0, c * qc:(c + 1) * qc, :]), EXP2_ARG_MAX)

    def log_weights(c, z2, mask):
        cols = slice(c * qc, (c + 1) * qc)
        u = jnp.log2(1.0 + jnp.exp2(z2))
        if mask is not None:
            u = jnp.where(mask, u, 0.0)
        u_bf = u.astype(BF16)
        run = run_ref[:, cols]
        log_a = [None] * (tk // sub)
        for sb in reversed(range(tk // sub)):
            rows = slice(sb * sub, (sb + 1) * sub)
            suffix = _dot(tri_ref[...], u_bf[rows, :])
            log_a[sb] = z2[rows, :] - suffix - run
            run = run + suffix[0:1, :]
        run_ref[:, cols] = run
        return jnp.concatenate(log_a, axis=0)

    def accumulate(c, log_a, vt, mask):
        a = jnp.exp2(log_a)
        if mask is not None:
            a = jnp.where(mask, a, 0.0)
        acc_ref[:, c * qc:(c + 1) * qc] += _dot(vt, a.astype(BF16))

    def key_block(ks, vt, chunk_masks):
        z2 = {c: scores(c, ks) for c in chunk_masks}
        log_a = {c: log_weights(c, z2[c], chunk_masks[c]) for c in chunk_masks}
        for c in chunk_masks:
            accumulate(c, log_a[c], vt, chunk_masks[c])

    key = lax.broadcasted_iota(jnp.int32, (tk, qc), 0)
    qry = lax.broadcasted_iota(jnp.int32, (tk, qc), 1)
    for kb in reversed(range(tq // tk)):
        ks, vt = load_kv(i * (tq // tk) + kb)
        chunk_masks = {}
        for c in range(tq // qc):
            validity = _tile_validity(kb * tk, tk, c * qc, qc, strict=True)
            if validity != "none":
                chunk_masks[c] = None if validity == "all" else key + kb * tk < qry + c * qc
        key_block(ks, vt, chunk_masks)

    def body(jj, carry):
        ks, vt = load_kv(i * (tq // tk) - 1 - jj)
        key_block(ks, vt, {c: None for c in range(tq // qc)})
        return carry

    lax.fori_loop(0, i * (tq // tk), body, 0)
    o_ref[0] = acc_ref[...].T.astype(o_ref.dtype)


def _sb_attention(q, k, vt, *, tq):
    b, s, d = q.shape
    tk = vt.shape[3]
    heads = d // HEAD_DIM
    qc = min(tq, V7X_MXUS * V7X_MXU_DIM)
    sub = min(tk, V7X_MXU_DIM)
    r = jnp.arange(sub)
    tri = (r[None, :] >= r[:, None]).astype(BF16)
    est = 2 * s * HEAD_DIM * 2 + 12 * (tq // qc) * tk * qc * 4 + 4 * HEAD_DIM * tq * 4
    return pl.pallas_call(
        functools.partial(_sb_kernel, tq=tq, tk=tk, qc=qc, sub=sub),
        grid=(b, heads, s // tq),
        in_specs=[
            pl.BlockSpec((1, tq, HEAD_DIM), lambda bi, h, i: (bi, i, h)),
            pl.BlockSpec((1, s, HEAD_DIM), lambda bi, h, i: (bi, 0, h), pipeline_mode=pl.Buffered(1)),
            pl.BlockSpec((1, s // tk, HEAD_DIM, tk), lambda bi, h, i: (bi, 0, h, 0),
                         pipeline_mode=pl.Buffered(1)),
            _const_spec((sub, sub)),
        ],
        out_specs=pl.BlockSpec((1, tq, HEAD_DIM), lambda bi, h, i: (bi, i, h)),
        out_shape=jax.ShapeDtypeStruct((b, s, d), BF16),
        scratch_shapes=[pltpu.VMEM((1, tq), F32), pltpu.VMEM((HEAD_DIM, tq), F32)],
        compiler_params=pltpu.CompilerParams(
            dimension_semantics=("arbitrary", "arbitrary", "arbitrary"),
            vmem_limit_bytes=_vmem_limit(est)),
        name="sb_attention",
    )(q, k, vt, tri)


def _diff_kernel(slope2_ref, lq1_ref, lk1_ref, lq2_ref, lk2_ref, g_ref, qf_ref, kf_ref,
                 q1_ref, q2_ref, k1_ref, k2_ref, vt_ref, o_ref, m_ref, l_ref, acc_ref,
                 *, tq, tk, qc, lambda_init):
    h = pl.program_id(1)
    i = pl.program_id(2)
    slope2 = slope2_ref[h]
    m_ref[...] = jnp.full_like(m_ref, MASK_VALUE)
    l_ref[...] = jnp.zeros_like(l_ref)
    acc_ref[...] = jnp.zeros_like(acc_ref)
    qf = jnp.broadcast_to(qf_ref[0], (qc, qf_ref.shape[2]))
    q_refs = (q1_ref, q2_ref)
    k_refs = (k1_ref, k2_ref)

    def load_kv(jb):
        start = pl.multiple_of(jb * tk, tk)
        k_aug = [jnp.concatenate([k_refs[mp][0, pl.ds(start, tk), :], kf_ref[...]], axis=1)
                 for mp in range(2)]
        return k_aug, vt_ref[0, jb]

    def scores(mp, c, k_aug):
        q_aug = jnp.concatenate([q_refs[mp][0, c * qc:(c + 1) * qc, :], qf], axis=1)
        return _dot_nt(k_aug, q_aug)

    def probabilities(mp, c, s, off, mask):
        cols = slice(c * qc, (c + 1) * qc)
        if mask is not None:
            s = jnp.where(mask, s, MASK_VALUE)
        m = m_ref[mp, :, cols]
        m_new = jnp.maximum(m, _col_reduce(s, jnp.maximum, jnp.max) + off)
        alpha = jnp.exp2(m - m_new)
        p = jnp.exp2(s - (m_new - off))
        m_ref[mp, :, cols] = m_new
        l_ref[mp, :, cols] = alpha * l_ref[mp, :, cols] + _col_reduce(p, jnp.add, jnp.sum)
        return alpha, p.astype(BF16)

    def accumulate(mp, c, alpha, p_bf, vt):
        cols = slice(c * qc, (c + 1) * qc)
        acc_ref[mp, :, cols] = alpha * acc_ref[mp, :, cols] + _dot(vt, p_bf)

    def key_block(k_aug, vt, off, chunk_masks):
        tiles = [(mp, c) for c in chunk_masks for mp in range(2)]
        s = {t: scores(*t, k_aug[t[0]]) for t in tiles}
        ap = {t: probabilities(*t, s[t], off, chunk_masks[t[1]]) for t in tiles}
        for t in tiles:
            accumulate(*t, *ap[t], vt)

    key = lax.broadcasted_iota(jnp.int32, (tk, qc), 0)
    qry = lax.broadcasted_iota(jnp.int32, (tk, qc), 1)
    for kb in reversed(range(tq // tk)):
        k_aug, vt = load_kv(i * (tq // tk) + kb)
        chunk_masks = {}
        for c in range(tq // qc):
            validity = _tile_validity(kb * tk, tk, c * qc, qc, strict=False)
            if validity != "none":
                chunk_masks[c] = None if validity == "all" else key + kb * tk <= qry + c * qc
        key_block(k_aug, vt, slope2 * (kb * tk), chunk_masks)

    def body(jj, carry):
        j = i * (tq // tk) - 1 - jj
        k_aug, vt = load_kv(j)
        off = slope2 * (j * tk - i * tq).astype(F32)
        key_block(k_aug, vt, off, {c: None for c in range(tq // qc)})
        return carry

    lax.fori_loop(0, i * (tq // tk), body, 0)

    lam = (jnp.exp(jnp.sum(lq1_ref[...] * lk1_ref[...], axis=1, keepdims=True))
           - jnp.exp(jnp.sum(lq2_ref[...] * lk2_ref[...], axis=1, keepdims=True))
           + lambda_init)
    o_t = acc_ref[0] / l_ref[0] - lam * (acc_ref[1] / l_ref[1])
    ms = jnp.mean(o_t * o_t, axis=0, keepdims=True)
    o_t = o_t * lax.rsqrt(ms + HEAD_NORM_EPS) * (g_ref[...] * (1.0 - lambda_init))
    o_ref[0] = o_t.T.astype(o_ref.dtype)


def _alibi_features(slope2, blk):
    pieces = []
    rest = slope2
    for _ in range(ALIBI_SLOPE_PIECES):
        piece = rest.astype(BF16)
        pieces.append(piece)
        rest = rest - piece.astype(F32)
    qf = jnp.stack([p * r for p in pieces for r in (ALIBI_POS_RADIX, 1)], axis=1)
    pos = jnp.arange(blk)
    digits = jnp.stack([pos // ALIBI_POS_RADIX, pos % ALIBI_POS_RADIX] * ALIBI_SLOPE_PIECES, axis=1)
    pad = HEAD_DIM - 2 * ALIBI_SLOPE_PIECES
    qf = jnp.pad(qf.astype(BF16), ((0, 0), (0, pad)))[:, None, :]
    kf = jnp.pad(digits.astype(BF16), ((0, 0), (0, pad)))
    return qf, kf


def _diff_attention(q, k, vt, lq1, lk1, lq2, lk2, g_head, *, layer_idx, tq):
    b, s, d = q.shape
    tk = vt.shape[3]
    heads = d // DIFF_V_DIM
    qc = min(tq, V7X_MXUS * V7X_MXU_DIM)
    assert tk <= ALIBI_POS_RADIX * 256
    lambda_init = 0.8 - 0.6 * math.exp(-0.3 * layer_idx)
    slope2 = LOG2_E * 2.0 ** (-8.0 * jnp.arange(1, heads + 1, dtype=F32) / heads)
    qf, kf = _alibi_features(slope2, tk)
    vec = lambda a: a.reshape(1, -1).astype(F32)
    est = (s * (2 * HEAD_DIM + DIFF_V_DIM) * 2 + 12 * 2 * (tq // qc) * tk * qc * 4
           + 4 * tq * DIFF_V_DIM * 4)
    return pl.pallas_call(
        functools.partial(_diff_kernel, tq=tq, tk=tk, qc=qc, lambda_init=lambda_init),
        grid=(b, heads, s // tq),
        in_specs=[
            pl.BlockSpec(memory_space=pltpu.SMEM),
            _const_spec((1, HEAD_DIM)), _const_spec((1, HEAD_DIM)),
            _const_spec((1, HEAD_DIM)), _const_spec((1, HEAD_DIM)),
            _const_spec((DIFF_V_DIM, 1)),
            pl.BlockSpec((1, 1, HEAD_DIM), lambda bi, h, i: (h, 0, 0)),
            _const_spec((tk, HEAD_DIM)),
            pl.BlockSpec((1, tq, HEAD_DIM), lambda bi, h, i: (bi, i, 2 * h)),
            pl.BlockSpec((1, tq, HEAD_DIM), lambda bi, h, i: (bi, i, 2 * h + 1)),
            pl.BlockSpec((1, s, HEAD_DIM), lambda bi, h, i: (bi, 0, 2 * h),
                         pipeline_mode=pl.Buffered(1)),
            pl.BlockSpec((1, s, HEAD_DIM), lambda bi, h, i: (bi, 0, 2 * h + 1),
                         pipeline_mode=pl.Buffered(1)),
            pl.BlockSpec((1, s // tk, DIFF_V_DIM, tk), lambda bi, h, i: (bi, 0, h, 0),
                         pipeline_mode=pl.Buffered(1)),
        ],
        out_specs=pl.BlockSpec((1, tq, DIFF_V_DIM), lambda bi, h, i: (bi, i, h)),
        out_shape=jax.ShapeDtypeStruct((b, s, d), BF16),
        scratch_shapes=[pltpu.VMEM((2, 1, tq), F32), pltpu.VMEM((2, 1, tq), F32),
                        pltpu.VMEM((2, DIFF_V_DIM, tq), F32)],
        compiler_params=pltpu.CompilerParams(
            dimension_semantics=("arbitrary", "arbitrary", "arbitrary"),
            vmem_limit_bytes=_vmem_limit(est)),
        name="diff_attention",
    )(slope2, vec(lq1), vec(lk1), vec(lq2), vec(lk2), g_head.reshape(-1, 1).astype(F32), qf, kf,
      q, q, k, k, vt)


def _post_kernel(mix_ref, mq_ref, x_ref, km_ref, vm_ref, wo_ref, g_ref, w1_ref, w2_ref, gf_ref,
                 o_ref, *, ff_chunks, final_norm):
    tm, d = x_ref.shape
    mw = mq_ref.shape[1]
    d_ff = w2_ref.shape[0]
    head_dim = mw // MEM_HEADS

    mq = mq_ref[...]
    km = km_ref[0]
    vm = vm_ref[0]
    lane_head = lax.broadcasted_iota(jnp.int32, (tm, mw), 1) // head_dim
    mo = jnp.zeros((tm, mw), F32)
    for hd in range(MEM_HEADS):
        in_head = lane_head == hd
        s = _dot_nt(jnp.where(in_head, mq, jnp.zeros_like(mq)), km)
        p = jnp.exp(s - jnp.max(s, axis=1, keepdims=True))
        l = jnp.sum(p, axis=1, keepdims=True)
        mo = jnp.where(in_head, _dot(p.astype(BF16), vm) / l, mo)

    y = x_ref[...] + _dot(mix_ref[...], wo_ref[0:d, :]) + _dot(mo.astype(BF16), wo_ref[d:, :])

    yn = _rms(y, g_ref[...], NORM_EPS).astype(BF16)
    hidden = []
    for lo, hi in ff_chunks:
        gate = _dot(yn, w1_ref[:, lo:hi])
        up = _dot(yn, w1_ref[:, d_ff + lo:d_ff + hi])
        hidden.append((gate * (1.0 / (1.0 + jnp.exp(-gate))) * up).astype(BF16))
    y = y + _dot(jnp.concatenate(hidden, axis=1), w2_ref[...])
    if final_norm:
        y = _rms(y, gf_ref[...], NORM_EPS)
    o_ref[...] = y


def _ff_chunks(d_ff, n_chunks, align):
    tiles = d_ff // align
    bounds = [align * ((tiles * c) // n_chunks) for c in range(n_chunks)] + [d_ff]
    return tuple((bounds[c], bounds[c + 1]) for c in range(n_chunks))


def _post(mix2d, mq2d, x2d, km, vm, wo, g_ffn, w1, w2, g_final, *, seq, tm, final_norm):
    t, d = x2d.shape
    mw = mq2d.shape[1]
    mem_len = km.shape[1]
    d_ff = w2.shape[0]
    tiles_per_seq = seq // tm
    ff_chunks = _ff_chunks(d_ff, 2, 256)
    chunk = max(hi - lo for lo, hi in ff_chunks)
    est = ((wo.size + w1.size + w2.size) * 2 + 2 * tm * (d + mw) * 2 + 4 * tm * d * 4
           + tm * chunk * 12 + 4 * tm * d * 4)
    return pl.pallas_call(
        functools.partial(_post_kernel, ff_chunks=ff_chunks, final_norm=final_norm),
        grid=(t // tm,),
        in_specs=[
            pl.BlockSpec((tm, d), lambda i: (i, 0)),
            pl.BlockSpec((tm, mw), lambda i: (i, 0)),
            pl.BlockSpec((tm, d), lambda i: (i, 0)),
            pl.BlockSpec((1, mem_len, mw), lambda i: (i // tiles_per_seq, 0, 0)),
            pl.BlockSpec((1, mem_len, mw), lambda i: (i // tiles_per_seq, 0, 0)),
            _const_spec(wo.shape),
            _const_spec((1, d)),
            _const_spec(w1.shape),
            _const_spec(w2.shape),
            _const_spec((1, d)),
        ],
        out_specs=pl.BlockSpec((tm, d), lambda i: (i, 0)),
        out_shape=jax.ShapeDtypeStruct((t, d), F32),
        compiler_params=pltpu.CompilerParams(
            dimension_semantics=("arbitrary",), vmem_limit_bytes=_vmem_limit(est)),
        name="post",
    )(mix2d, mq2d, x2d, km, vm, wo, g_ffn.reshape(1, d), w1, w2, g_final.reshape(1, d))


def kernel(x, mem, g_mix, w_in, w_mem_kv, w_o, g_ffn, w_ffn_in, w_ffn_out,
           lam_q1, lam_k1, lam_q2, lam_k2, g_diff_head, conv_w, g_mem, g_final):
    b, seq, d = x.shape
    mem_len = mem.shape[1]
    depth = w_in.shape[0]
    t = b * seq
    tm = min(512, seq)
    sb_tq = min(1024, seq)
    diff_tq = min(2048, seq)

    km_all, vm_all = _mem_kv(mem.reshape(b * mem_len, d), g_mem, w_mem_kv.astype(BF16))
    mw = km_all.shape[-1]
    km_all = km_all.reshape(depth, b, mem_len, mw)
    vm_all = vm_all.reshape(depth, b, mem_len, mw)

    x2d = x.reshape(t, d)
    for i in range(depth):
        kind = i % N_MIXERS
        j = i // N_MIXERS
        w = w_in[i].astype(BF16)
        if kind == 2:
            mix, mq = _conv_proj(x2d, g_mix[i], w, conv_w[j], seq=seq, tm=tm)
        else:
            q, k, vt, mq = _in_proj(x2d, g_mix[i], w, q_scale=LOG2_E * HEAD_DIM ** -0.5, tm=tm)
            q, k = (a.reshape(b, seq, d) for a in (q, k))
            vt = vt.reshape(b, seq // tm, d, tm)
            if kind == 0:
                mix = _sb_attention(q, k, vt, tq=sb_tq)
            else:
                mix = _diff_attention(q, k, vt, lam_q1[j], lam_k1[j], lam_q2[j], lam_k2[j],
                                      g_diff_head[j], layer_idx=i, tq=diff_tq)
            mix = mix.reshape(t, d)
        x2d = _post(mix, mq, x2d, km_all[i], vm_all[i], w_o[i].astype(BF16), g_ffn[i],
                    w_ffn_in[i].astype(BF16), w_ffn_out[i].astype(BF16), g_final,
                    seq=seq, tm=tm, final_norm=(i == depth - 1))
    return x2d.reshape(b, seq, d)
```

```python
import functools
import math

import jax
import jax.numpy as jnp
from jax import lax
from jax.experimental import pallas as pl
from jax.experimental.pallas import tpu as pltpu

F32 = jnp.float32
BF16 = jnp.bfloat16

N_MIXERS = 3
HEAD_DIM = 128
DIFF_V_DIM = 2 * HEAD_DIM
MEM_HEADS = 4
CONV_WIDTH = 3
NORM_EPS = 1e-6
HEAD_NORM_EPS = 1e-5
MASK_VALUE = -1e30
EXP2_ARG_MAX = 126.0
ALIBI_SLOPE_PIECES = 3
ALIBI_POS_RADIX = 16

V7X_VMEM_BYTES = 64 * 1024 * 1024
V7X_MXU_DIM = 256
V7X_MXUS = 2
LOG2_E = 1.4426950408889634
SUBLANES = 8
COL_REDUCE_WAYS = 8
MATMUL_LOOKAHEAD = 2


def _vmem_limit(estimate_bytes):
    return int(min(max(estimate_bytes, 16 * 1024 * 1024), V7X_VMEM_BYTES - 8 * 1024 * 1024))


def _rms(xf, g, eps):
    ms = jnp.mean(xf * xf, axis=-1, keepdims=True)
    return xf * lax.rsqrt(ms + eps) * g


def _dot(a, b):
    return jnp.dot(a, b, preferred_element_type=F32)


def _dot_nt(a, b):
    return lax.dot_general(a, b, (((1,), (1,)), ((), ())), preferred_element_type=F32)


def _const_spec(shape):
    nd = len(shape)
    return pl.BlockSpec(shape, lambda *_: (0,) * nd, pipeline_mode=pl.Buffered(1))


def _mem_kv_kernel(mem_ref, g_ref, w_ref, k_ref, v_ref, *, mem_width):
    mem_n = _rms(mem_ref[...], g_ref[...], NORM_EPS).astype(BF16)
    kv = _dot(mem_n, w_ref[0])
    head_dim = mem_width // MEM_HEADS
    k_ref[0] = (kv[:, :mem_width] * head_dim ** -0.5).astype(BF16)
    v_ref[0] = kv[:, mem_width:].astype(BF16)


def _mem_kv(mem2d, g_mem, w_kv):
    depth, d_model, two_w = w_kv.shape
    mem_width = two_w // 2
    rows = mem2d.shape[0]
    return pl.pallas_call(
        functools.partial(_mem_kv_kernel, mem_width=mem_width),
        grid=(depth,),
        in_specs=[
            pl.BlockSpec((rows, d_model), lambda i: (0, 0)),
            pl.BlockSpec((1, d_model), lambda i: (0, 0)),
            pl.BlockSpec((1, d_model, two_w), lambda i: (i, 0, 0)),
        ],
        out_specs=[
            pl.BlockSpec((1, rows, mem_width), lambda i: (i, 0, 0)),
            pl.BlockSpec((1, rows, mem_width), lambda i: (i, 0, 0)),
        ],
        out_shape=[jax.ShapeDtypeStruct((depth, rows, mem_width), BF16)] * 2,
        name="mem_kv",
    )(mem2d, g_mem.reshape(1, d_model), w_kv)


def _in_proj_kernel(x_ref, g_ref, w_ref, q_ref, k_ref, vt_ref, mq_ref, *, q_scale):
    d = x_ref.shape[1]
    hn = _rms(x_ref[...], g_ref[...], NORM_EPS).astype(BF16)
    q_ref[...] = (_dot(hn, w_ref[:, 0:d]) * q_scale).astype(BF16)
    k_ref[...] = _dot(hn, w_ref[:, d:2 * d]).astype(BF16)
    vt_ref[0] = _dot(hn, w_ref[:, 2 * d:3 * d]).T.astype(BF16)
    mq_ref[...] = _dot(hn, w_ref[:, 3 * d:]).astype(BF16)


def _in_proj(x2d, g, w, *, q_scale, tm):
    t, d = x2d.shape
    n = w.shape[1]
    mw = n - 3 * d
    est = 2 * tm * d * 4 + d * n * 2 + 2 * tm * n * 2 + 3 * tm * d * 4
    return pl.pallas_call(
        functools.partial(_in_proj_kernel, q_scale=q_scale),
        grid=(t // tm,),
        in_specs=[
            pl.BlockSpec((tm, d), lambda i: (i, 0)),
            _const_spec((1, d)),
            _const_spec((d, n)),
        ],
        out_specs=[
            pl.BlockSpec((tm, d), lambda i: (i, 0)),
            pl.BlockSpec((tm, d), lambda i: (i, 0)),
            pl.BlockSpec((1, d, tm), lambda i: (i, 0, 0)),
            pl.BlockSpec((tm, mw), lambda i: (i, 0)),
        ],
        out_shape=[jax.ShapeDtypeStruct((t, d), BF16)] * 2
        + [jax.ShapeDtypeStruct((t // tm, d, tm), BF16), jax.ShapeDtypeStruct((t, mw), BF16)],
        compiler_params=pltpu.CompilerParams(
            dimension_semantics=("arbitrary",), vmem_limit_bytes=_vmem_limit(est)),
        name="in_proj",
    )(x2d, g.reshape(1, d), w)


def _conv_proj_kernel(x_ref, g_ref, w_ref, cw_ref, mix_ref, mq_ref, u_ref, *, tiles_per_seq):
    tm, d = x_ref.shape
    i = pl.program_id(0)

    @pl.when(i % tiles_per_seq == 0)
    def _():
        u_ref[0:SUBLANES, :] = jnp.zeros((SUBLANES, d), F32)

    hn = _rms(x_ref[...], g_ref[...], NORM_EPS).astype(BF16)
    gate_c = _dot(hn, w_ref[:, d:2 * d])
    h = _dot(hn, w_ref[:, 2 * d:3 * d])
    u_ref[SUBLANES:SUBLANES + tm, :] = gate_c * h
    y = cw_ref[CONV_WIDTH - 1:CONV_WIDTH, :] * u_ref[SUBLANES:SUBLANES + tm, :]
    for tap in range(CONV_WIDTH - 1):
        shift = CONV_WIDTH - 1 - tap
        y = y + cw_ref[tap:tap + 1, :] * u_ref[SUBLANES - shift:SUBLANES - shift + tm, :]
    gate_b = _dot(hn, w_ref[:, 0:d])
    mix_ref[...] = (gate_b * y).astype(BF16)
    mq_ref[...] = _dot(hn, w_ref[:, 3 * d:]).astype(BF16)
    u_ref[0:SUBLANES, :] = u_ref[tm:tm + SUBLANES, :]


def _conv_proj(x2d, g, w, conv_w, *, seq, tm):
    t, d = x2d.shape
    n = w.shape[1]
    mw = n - 3 * d
    est = 2 * tm * d * 4 + d * n * 2 + 2 * tm * (d + mw) * 2 + 5 * tm * d * 4
    return pl.pallas_call(
        functools.partial(_conv_proj_kernel, tiles_per_seq=seq // tm),
        grid=(t // tm,),
        in_specs=[
            pl.BlockSpec((tm, d), lambda i: (i, 0)),
            _const_spec((1, d)),
            _const_spec((d, n)),
            _const_spec((CONV_WIDTH, d)),
        ],
        out_specs=[
            pl.BlockSpec((tm, d), lambda i: (i, 0)),
            pl.BlockSpec((tm, mw), lambda i: (i, 0)),
        ],
        out_shape=[jax.ShapeDtypeStruct((t, d), BF16), jax.ShapeDtypeStruct((t, mw), BF16)],
        scratch_shapes=[pltpu.VMEM((tm + SUBLANES, d), F32)],
        compiler_params=pltpu.CompilerParams(
            dimension_semantics=("arbitrary",), vmem_limit_bytes=_vmem_limit(est)),
        name="conv_proj",
    )(x2d, g.reshape(1, d), w, conv_w)


def _col_reduce(x, op, reduce_fn):
    rows = x.shape[0]
    parts = [x[r * (rows // COL_REDUCE_WAYS):(r + 1) * (rows // COL_REDUCE_WAYS), :]
             for r in range(COL_REDUCE_WAYS)]
    while len(parts) > 1:
        parts = [op(parts[2 * r], parts[2 * r + 1]) for r in range(len(parts) // 2)]
    return reduce_fn(parts[0], axis=0, keepdims=True)


def _tile_validity(key_off, n_keys, qry_off, n_qry, strict):
    last_ok = key_off + n_keys - 1 < qry_off if strict else key_off + n_keys - 1 <= qry_off
    first_bad = key_off >= qry_off + n_qry - 1 if strict else key_off > qry_off + n_qry - 1
    return "all" if last_ok else ("none" if first_bad else "some")


def _sb_kernel(q_ref, k_ref, vt_ref, tri_ref, o_ref, run_ref, acc_ref, z_ref, *, tq, tk, qc, sub):
    i = pl.program_id(2)
    run_ref[...] = jnp.zeros_like(run_ref)
    acc_ref[...] = jnp.zeros_like(acc_ref)

    def load_kv(jb):
        return k_ref[0, pl.ds(pl.multiple_of(jb * tk, tk), tk), :], vt_ref[0, jb]

    def scores(c, ks):
        return jnp.minimum(_dot_nt(ks, q_ref[0, c * qc:(c + 1) * qc, :]), EXP2_ARG_MAX)

    def log_weights(c, z2, mask):
        cols = slice(c * qc, (c + 1) * qc)
        u = jnp.log2(1.0 + jnp.exp2(z2))
        if mask is not None:
            u = jnp.where(mask, u, 0.0)
        u_bf = u.astype(BF16)
        run = run_ref[:, cols]
        log_a = [None] * (tk // sub)
        for sb in reversed(range(tk // sub)):
            rows = slice(sb * sub, (sb + 1) * sub)
            suffix = _dot(tri_ref[...], u_bf[rows, :])
            log_a[sb] = z2[rows, :] - suffix - run
            run = run + suffix[0:1, :]
        run_ref[:, cols] = run
        return jnp.concatenate(log_a, axis=0)

    def accumulate(c, log_a, vt, mask):
        a = jnp.exp2(log_a)
        if mask is not None:
            a = jnp.where(mask, a, 0.0)
        acc_ref[:, c * qc:(c + 1) * qc] += _dot(vt, a.astype(BF16))

    def key_block(ks, vt, chunk_masks):
        z2 = {c: scores(c, ks) for c in chunk_masks}
        log_a = {c: log_weights(c, z2[c], chunk_masks[c]) for c in chunk_masks}
        for c in chunk_masks:
            accumulate(c, log_a[c], vt, chunk_masks[c])

    key = lax.broadcasted_iota(jnp.int32, (tk, qc), 0)
    qry = lax.broadcasted_iota(jnp.int32, (tk, qc), 1)
    for kb in reversed(range(tq // tk)):
        ks, vt = load_kv(i * (tq // tk) + kb)
        chunk_masks = {}
        for c in range(tq // qc):
            validity = _tile_validity(kb * tk, tk, c * qc, qc, strict=True)
            if validity != "none":
                chunk_masks[c] = None if validity == "all" else key + kb * tk < qry + c * qc
        key_block(ks, vt, chunk_masks)

    assert (tq // tk) % 2 == 0
    n_before = i * (tq // tk)
    chunks = range(tq // qc)

    def prefetch_scores(jb, slot):
        ks, _ = load_kv(jnp.maximum(jb, 0))
        for c in chunks:
            z_ref[slot, c] = scores(c, ks)

    def process(jb, slot):
        _, vt = load_kv(jb)
        log_a = {c: log_weights(c, z_ref[slot, c], None) for c in chunks}
        prefetch_scores(jb - 1, 1 - slot)
        for c in chunks:
            accumulate(c, log_a[c], vt, None)

    prefetch_scores(n_before - 1, 0)

    def body(jj, carry):
        jb = n_before - 1 - 2 * jj
        process(jb, 0)
        process(jb - 1, 1)
        return carry

    lax.fori_loop(0, n_before // 2, body, 0)
    o_ref[0] = acc_ref[...].T.astype(o_ref.dtype)


def _sb_attention(q, k, vt, *, tq):
    b, s, d = q.shape
    tk = vt.shape[3]
    heads = d // HEAD_DIM
    qc = min(tq, V7X_MXUS * V7X_MXU_DIM)
    sub = min(tk, V7X_MXU_DIM)
    r = jnp.arange(sub)
    tri = (r[None, :] >= r[:, None]).astype(BF16)
    est = 2 * s * HEAD_DIM * 2 + 12 * (tq // qc) * tk * qc * 4 + 4 * HEAD_DIM * tq * 4
    return pl.pallas_call(
        functools.partial(_sb_kernel, tq=tq, tk=tk, qc=qc, sub=sub),
        grid=(b, heads, s // tq),
        in_specs=[
            pl.BlockSpec((1, tq, HEAD_DIM), lambda bi, h, i: (bi, i, h)),
            pl.BlockSpec((1, s, HEAD_DIM), lambda bi, h, i: (bi, 0, h), pipeline_mode=pl.Buffered(1)),
            pl.BlockSpec((1, s // tk, HEAD_DIM, tk), lambda bi, h, i: (bi, 0, h, 0),
                         pipeline_mode=pl.Buffered(1)),
            _const_spec((sub, sub)),
        ],
        out_specs=pl.BlockSpec((1, tq, HEAD_DIM), lambda bi, h, i: (bi, i, h)),
        out_shape=jax.ShapeDtypeStruct((b, s, d), BF16),
        scratch_shapes=[pltpu.VMEM((1, tq), F32), pltpu.VMEM((HEAD_DIM, tq), F32),
                        pltpu.VMEM((2, tq // qc, tk, qc), F32)],
        compiler_params=pltpu.CompilerParams(
            dimension_semantics=("arbitrary", "arbitrary", "arbitrary"),
            vmem_limit_bytes=_vmem_limit(est)),
        name="sb_attention",
    )(q, k, vt, tri)


def _diff_kernel(slope2_ref, lq1_ref, lk1_ref, lq2_ref, lk2_ref, g_ref, qf_ref, kf_ref,
                 q1_ref, q2_ref, k1_ref, k2_ref, vt_ref, o_ref, m_ref, l_ref, acc_ref, s_ref,
                 *, tq, tk, qc, lambda_init):
    h = pl.program_id(1)
    i = pl.program_id(2)
    slope2 = slope2_ref[h]
    m_ref[...] = jnp.full_like(m_ref, MASK_VALUE)
    l_ref[...] = jnp.zeros_like(l_ref)
    acc_ref[...] = jnp.zeros_like(acc_ref)
    qf = jnp.broadcast_to(qf_ref[0], (qc, qf_ref.shape[2]))
    q_refs = (q1_ref, q2_ref)
    k_refs = (k1_ref, k2_ref)

    def load_kv(jb):
        start = pl.multiple_of(jb * tk, tk)
        k_aug = [jnp.concatenate([k_refs[mp][0, pl.ds(start, tk), :], kf_ref[...]], axis=1)
                 for mp in range(2)]
        return k_aug, vt_ref[0, jb]

    def scores(mp, c, k_aug):
        q_aug = jnp.concatenate([q_refs[mp][0, c * qc:(c + 1) * qc, :], qf], axis=1)
        return _dot_nt(k_aug, q_aug)

    def probabilities(mp, c, s, off, mask):
        cols = slice(c * qc, (c + 1) * qc)
        if mask is not None:
            s = jnp.where(mask, s, MASK_VALUE)
        m = m_ref[mp, :, cols]
        m_new = jnp.maximum(m, _col_reduce(s, jnp.maximum, jnp.max) + off)
        alpha = jnp.exp2(m - m_new)
        p = jnp.exp2(s - (m_new - off))
        m_ref[mp, :, cols] = m_new
        l_ref[mp, :, cols] = alpha * l_ref[mp, :, cols] + _col_reduce(p, jnp.add, jnp.sum)
        return alpha, p.astype(BF16)

    def accumulate(mp, c, alpha, p_bf, vt):
        cols = slice(c * qc, (c + 1) * qc)
        acc_ref[mp, :, cols] = alpha * acc_ref[mp, :, cols] + _dot(vt, p_bf)

    def key_block(k_aug, vt, off, chunk_masks):
        tiles = [(mp, c) for c in chunk_masks for mp in range(2)]
        s = {t: scores(*t, k_aug[t[0]]) for t in tiles[:MATMUL_LOOKAHEAD]}
        for n, t in enumerate(tiles):
            ap = probabilities(*t, s.pop(t), off, chunk_masks[t[1]])
            if n + MATMUL_LOOKAHEAD < len(tiles):
                ahead = tiles[n + MATMUL_LOOKAHEAD]
                s[ahead] = scores(*ahead, k_aug[ahead[0]])
            accumulate(*t, *ap, vt)

    key = lax.broadcasted_iota(jnp.int32, (tk, qc), 0)
    qry = lax.broadcasted_iota(jnp.int32, (tk, qc), 1)
    for kb in reversed(range(tq // tk)):
        k_aug, vt = load_kv(i * (tq // tk) + kb)
        chunk_masks = {}
        for c in range(tq // qc):
            validity = _tile_validity(kb * tk, tk, c * qc, qc, strict=False)
            if validity != "none":
                chunk_masks[c] = None if validity == "all" else key + kb * tk <= qry + c * qc
        key_block(k_aug, vt, slope2 * (kb * tk), chunk_masks)

    assert (tq // tk) % 2 == 0
    n_before = i * (tq // tk)
    tiles = [(mp, c) for c in range(tq // qc) for mp in range(2)]

    def prefetch_scores(jb, slot):
        k_aug, _ = load_kv(jnp.maximum(jb, 0))
        for n, t in enumerate(tiles):
            s_ref[slot, n] = scores(*t, k_aug[t[0]])

    def process(jb, slot):
        _, vt = load_kv(jb)
        off = slope2 * (jb * tk - i * tq).astype(F32)
        ap = [probabilities(*t, s_ref[slot, n], off, None) for n, t in enumerate(tiles)]
        prefetch_scores(jb - 1, 1 - slot)
        for n, t in enumerate(tiles):
            accumulate(*t, *ap[n], vt)

    prefetch_scores(n_before - 1, 0)

    def body(jj, carry):
        jb = n_before - 1 - 2 * jj
        process(jb, 0)
        process(jb - 1, 1)
        return carry

    lax.fori_loop(0, n_before // 2, body, 0)

    lam =(jnp.exp(jnp.sum(lq1_ref[...] * lk1_ref[...], axis=1, keepdims=True))
           - jnp.exp(jnp.sum(lq2_ref[...] * lk2_ref[...], axis=1, keepdims=True))
           + lambda_init)
    o_t = acc_ref[0] / l_ref[0] - lam * (acc_ref[1] / l_ref[1])
    ms = jnp.mean(o_t * o_t, axis=0, keepdims=True)
    o_t = o_t * lax.rsqrt(ms + HEAD_NORM_EPS) * (g_ref[...] * (1.0 - lambda_init))
    o_ref[0] = o_t.T.astype(o_ref.dtype)


def _alibi_features(slope2, blk):
    pieces = []
    rest = slope2
    for _ in range(ALIBI_SLOPE_PIECES):
        piece = rest.astype(BF16)
        pieces.append(piece)
        rest = rest - piece.astype(F32)
    qf = jnp.stack([p * r for p in pieces for r in (ALIBI_POS_RADIX, 1)], axis=1)
    pos = jnp.arange(blk)
    digits = jnp.stack([pos // ALIBI_POS_RADIX, pos % ALIBI_POS_RADIX] * ALIBI_SLOPE_PIECES, axis=1)
    pad = HEAD_DIM - 2 * ALIBI_SLOPE_PIECES
    qf = jnp.pad(qf.astype(BF16), ((0, 0), (0, pad)))[:, None, :]
    kf = jnp.pad(digits.astype(BF16), ((0, 0), (0, pad)))
    return qf, kf


def _diff_attention(q, k, vt, lq1, lk1, lq2, lk2, g_head, *, layer_idx, tq):
    b, s, d = q.shape
    tk = vt.shape[3]
    heads = d // DIFF_V_DIM
    qc = min(tq, V7X_MXUS * V7X_MXU_DIM)
    assert tk <= ALIBI_POS_RADIX * 256
    lambda_init = 0.8 - 0.6 * math.exp(-0.3 * layer_idx)
    slope2 = LOG2_E * 2.0 ** (-8.0 * jnp.arange(1, heads + 1, dtype=F32) / heads)
    qf, kf = _alibi_features(slope2, tk)
    vec = lambda a: a.reshape(1, -1).astype(F32)
    est = (s * (2 * HEAD_DIM + DIFF_V_DIM) * 2 + 12 * 2 * (tq // qc) * tk * qc * 4
           + 4 * tq * DIFF_V_DIM * 4)
    return pl.pallas_call(
        functools.partial(_diff_kernel, tq=tq, tk=tk, qc=qc, lambda_init=lambda_init),
        grid=(b, heads, s // tq),
        in_specs=[
            pl.BlockSpec(memory_space=pltpu.SMEM),
            _const_spec((1, HEAD_DIM)), _const_spec((1, HEAD_DIM)),
            _const_spec((1, HEAD_DIM)), _const_spec((1, HEAD_DIM)),
            _const_spec((DIFF_V_DIM, 1)),
            pl.BlockSpec((1, 1, HEAD_DIM), lambda bi, h, i: (h, 0, 0)),
            _const_spec((tk, HEAD_DIM)),
            pl.BlockSpec((1, tq, HEAD_DIM), lambda bi, h, i: (bi, i, 2 * h)),
            pl.BlockSpec((1, tq, HEAD_DIM), lambda bi, h, i: (bi, i, 2 * h + 1)),
            pl.BlockSpec((1, s, HEAD_DIM), lambda bi, h, i: (bi, 0, 2 * h),
                         pipeline_mode=pl.Buffered(1)),
            pl.BlockSpec((1, s, HEAD_DIM), lambda bi, h, i: (bi, 0, 2 * h + 1),
                         pipeline_mode=pl.Buffered(1)),
            pl.BlockSpec((1, s // tk, DIFF_V_DIM, tk), lambda bi, h, i: (bi, 0, h, 0),
                         pipeline_mode=pl.Buffered(1)),
        ],
        out_specs=pl.BlockSpec((1, tq, DIFF_V_DIM), lambda bi, h, i: (bi, i, h)),
        out_shape=jax.ShapeDtypeStruct((b, s, d), BF16),
        scratch_shapes=[pltpu.VMEM((2, 1, tq), F32), pltpu.VMEM((2, 1, tq), F32),
                        pltpu.VMEM((2, DIFF_V_DIM, tq), F32),
                        pltpu.VMEM((2, 2 * (tq // qc), tk, qc), F32)],
        compiler_params=pltpu.CompilerParams(
            dimension_semantics=("arbitrary", "arbitrary", "arbitrary"),
            vmem_limit_bytes=_vmem_limit(est)),
        name="diff_attention",
    )(slope2, vec(lq1), vec(lk1), vec(lq2), vec(lk2), g_head.reshape(-1, 1).astype(F32), qf, kf,
      q, q, k, k, vt)


def _post_kernel(mix_ref, mq_ref, x_ref, km_ref, vm_ref, wo_ref, g_ref, w1_ref, w2_ref, gf_ref,
                 o_ref, *, ff_chunks, final_norm):
    tm, d = x_ref.shape
    mw = mq_ref.shape[1]
    d_ff = w2_ref.shape[0]
    head_dim = mw // MEM_HEADS

    mq = mq_ref[...]
    km = km_ref[0]
    vm = vm_ref[0]
    lane_head = lax.broadcasted_iota(jnp.int32, (tm, mw), 1) // head_dim
    mo = jnp.zeros((tm, mw), F32)
    for hd in range(MEM_HEADS):
        in_head = lane_head == hd
        s = _dot_nt(jnp.where(in_head, mq, jnp.zeros_like(mq)), km)
        p = jnp.exp(s - jnp.max(s, axis=1, keepdims=True))
        l = jnp.sum(p, axis=1, keepdims=True)
        mo = jnp.where(in_head, _dot(p.astype(BF16), vm) / l, mo)

    y = x_ref[...] + _dot(mix_ref[...], wo_ref[0:d, :]) + _dot(mo.astype(BF16), wo_ref[d:, :])

    yn = _rms(y, g_ref[...], NORM_EPS).astype(BF16)
    hidden = []
    for lo, hi in ff_chunks:
        gate = _dot(yn, w1_ref[:, lo:hi])
        up = _dot(yn, w1_ref[:, d_ff + lo:d_ff + hi])
        hidden.append((gate * (1.0 / (1.0 + jnp.exp(-gate))) * up).astype(BF16))
    y = y + _dot(jnp.concatenate(hidden, axis=1), w2_ref[...])
    if final_norm:
        y = _rms(y, gf_ref[...], NORM_EPS)
    o_ref[...] = y


def _ff_chunks(d_ff, n_chunks, align):
    tiles = d_ff // align
    bounds = [align * ((tiles * c) // n_chunks) for c in range(n_chunks)] + [d_ff]
    return tuple((bounds[c], bounds[c + 1]) for c in range(n_chunks))


def _post(mix2d, mq2d, x2d, km, vm, wo, g_ffn, w1, w2, g_final, *, seq, tm, final_norm):
    t, d = x2d.shape
    mw = mq2d.shape[1]
    mem_len = km.shape[1]
    d_ff = w2.shape[0]
    tiles_per_seq = seq // tm
    ff_chunks = _ff_chunks(d_ff, 2, 256)
    chunk = max(hi - lo for lo, hi in ff_chunks)
    est = ((wo.size + w1.size + w2.size) * 2 + 2 * tm * (d + mw) * 2 + 4 * tm * d * 4
           + tm * chunk * 12 + 4 * tm * d * 4)
    return pl.pallas_call(
        functools.partial(_post_kernel, ff_chunks=ff_chunks, final_norm=final_norm),
        grid=(t // tm,),
        in_specs=[
            pl.BlockSpec((tm, d), lambda i: (i, 0)),
            pl.BlockSpec((tm, mw), lambda i: (i, 0)),
            pl.BlockSpec((tm, d), lambda i: (i, 0)),
            pl.BlockSpec((1, mem_len, mw), lambda i: (i // tiles_per_seq, 0, 0)),
            pl.BlockSpec((1, mem_len, mw), lambda i: (i // tiles_per_seq, 0, 0)),
            _const_spec(wo.shape),
            _const_spec((1, d)),
            _const_spec(w1.shape),
            _const_spec(w2.shape),
            _const_spec((1, d)),
        ],
        out_specs=pl.BlockSpec((tm, d), lambda i: (i, 0)),
        out_shape=jax.ShapeDtypeStruct((t, d), F32),
        compiler_params=pltpu.CompilerParams(
            dimension_semantics=("arbitrary",), vmem_limit_bytes=_vmem_limit(est)),
        name="post",
    )(mix2d, mq2d, x2d, km, vm, wo, g_ffn.reshape(1, d), w1, w2, g_final.reshape(1, d))


def kernel(x, mem, g_mix, w_in, w_mem_kv, w_o, g_ffn, w_ffn_in, w_ffn_out,
           lam_q1, lam_k1, lam_q2, lam_k2, g_diff_head, conv_w, g_mem, g_final):
    b, seq, d = x.shape
    mem_len = mem.shape[1]
    depth = w_in.shape[0]
    t = b * seq
    tm = min(512, seq)
    sb_tq = min(1024, seq)
    diff_tq = min(1024, seq)

    km_all, vm_all = _mem_kv(mem.reshape(b * mem_len, d), g_mem, w_mem_kv.astype(BF16))
    mw = km_all.shape[-1]
    km_all = km_all.reshape(depth, b, mem_len, mw)
    vm_all = vm_all.reshape(depth, b, mem_len, mw)

    x2d = x.reshape(t, d)
    for i in range(depth):
        kind = i % N_MIXERS
        j = i // N_MIXERS
        w = w_in[i].astype(BF16)
        if kind == 2:
            mix, mq = _conv_proj(x2d, g_mix[i], w, conv_w[j], seq=seq, tm=tm)
        else:
            q, k, vt, mq = _in_proj(x2d, g_mix[i], w, q_scale=LOG2_E * HEAD_DIM ** -0.5, tm=tm)
            q, k = (a.reshape(b, seq, d) for a in (q, k))
            vt = vt.reshape(b, seq // tm, d, tm)
            if kind == 0:
                mix = _sb_attention(q, k, vt, tq=sb_tq)
            else:
                mix = _diff_attention(q, k, vt, lam_q1[j], lam_k1[j], lam_q2[j], lam_k2[j],
                                      g_diff_head[j], layer_idx=i, tq=diff_tq)
            mix = mix.reshape(t, d)
        x2d = _post(mix, mq, x2d, km_all[i], vm_all[i], w_o[i].astype(BF16), g_ffn[i],
                    w_ffn_in[i].astype(BF16), w_ffn_out[i].astype(BF16), g_final,
                    seq=seq, tm=tm, final_norm=(i == depth - 1))
    return x2d.reshape(b, seq, d)
```

```python
import functools
import math

import jax
import jax.numpy as jnp
from jax import lax
from jax.experimental import pallas as pl
from jax.experimental.pallas import tpu as pltpu

F32 = jnp.float32
BF16 = jnp.bfloat16

N_MIXERS = 3
HEAD_DIM = 128
DIFF_V_DIM = 2 * HEAD_DIM
MEM_HEADS = 4
CONV_WIDTH = 3
NORM_EPS = 1e-6
HEAD_NORM_EPS = 1e-5
MASK_VALUE = -1e30
EXP2_ARG_MAX = 126.0
F32_EXP2_UNDERFLOW_ARG = -150.0
ALIBI_SLOPE_PIECES = 3
ALIBI_POS_RADIX = 16

V7X_VMEM_BYTES = 64 * 1024 * 1024
V7X_MXU_DIM = 256
V7X_MXUS = 2
LOG2_E = 1.4426950408889634
SUBLANES = 8
COL_REDUCE_WAYS = 8
MATMUL_LOOKAHEAD = 2


def _vmem_limit(estimate_bytes):
    return int(min(max(estimate_bytes, 16 * 1024 * 1024), V7X_VMEM_BYTES - 8 * 1024 * 1024))


def _rms(xf, g, eps):
    ms = jnp.mean(xf * xf, axis=-1, keepdims=True)
    return xf * lax.rsqrt(ms + eps) * g


def _dot(a, b):
    return jnp.dot(a, b, preferred_element_type=F32)


def _dot_nt(a, b):
    return lax.dot_general(a, b, (((1,), (1,)), ((), ())), preferred_element_type=F32)


def _const_spec(shape):
    nd = len(shape)
    return pl.BlockSpec(shape, lambda *_: (0,) * nd, pipeline_mode=pl.Buffered(1))


def _mem_kv_kernel(mem_ref, g_ref, w_ref, k_ref, v_ref, *, mem_width):
    mem_n = _rms(mem_ref[...], g_ref[...], NORM_EPS).astype(BF16)
    kv = _dot(mem_n, w_ref[0])
    head_dim = mem_width // MEM_HEADS
    k_ref[0] = (kv[:, :mem_width] * head_dim ** -0.5).astype(BF16)
    v_ref[0] = kv[:, mem_width:].astype(BF16)


def _mem_kv(mem2d, g_mem, w_kv):
    depth, d_model, two_w = w_kv.shape
    mem_width = two_w // 2
    rows = mem2d.shape[0]
    return pl.pallas_call(
        functools.partial(_mem_kv_kernel, mem_width=mem_width),
        grid=(depth,),
        in_specs=[
            pl.BlockSpec((rows, d_model), lambda i: (0, 0)),
            pl.BlockSpec((1, d_model), lambda i: (0, 0)),
            pl.BlockSpec((1, d_model, two_w), lambda i: (i, 0, 0)),
        ],
        out_specs=[
            pl.BlockSpec((1, rows, mem_width), lambda i: (i, 0, 0)),
            pl.BlockSpec((1, rows, mem_width), lambda i: (i, 0, 0)),
        ],
        out_shape=[jax.ShapeDtypeStruct((depth, rows, mem_width), BF16)] * 2,
        name="mem_kv",
    )(mem2d, g_mem.reshape(1, d_model), w_kv)


def _in_proj_kernel(x_ref, g_ref, w_ref, q_ref, k_ref, vt_ref, mq_ref, *, q_scale):
    d = x_ref.shape[1]
    hn = _rms(x_ref[...], g_ref[...], NORM_EPS).astype(BF16)
    q_ref[...] = (_dot(hn, w_ref[:, 0:d]) * q_scale).astype(BF16)
    k_ref[...] = _dot(hn, w_ref[:, d:2 * d]).astype(BF16)
    vt_ref[0] = _dot(hn, w_ref[:, 2 * d:3 * d]).T.astype(BF16)
    mq_ref[...] = _dot(hn, w_ref[:, 3 * d:]).astype(BF16)


def _in_proj(x2d, g, w, *, q_scale, tm):
    t, d = x2d.shape
    n = w.shape[1]
    mw = n - 3 * d
    est = 2 * tm * d * 4 + d * n * 2 + 2 * tm * n * 2 + 3 * tm * d * 4
    return pl.pallas_call(
        functools.partial(_in_proj_kernel, q_scale=q_scale),
        grid=(t // tm,),
        in_specs=[
            pl.BlockSpec((tm, d), lambda i: (i, 0)),
            _const_spec((1, d)),
            _const_spec((d, n)),
        ],
        out_specs=[
            pl.BlockSpec((tm, d), lambda i: (i, 0)),
            pl.BlockSpec((tm, d), lambda i: (i, 0)),
            pl.BlockSpec((1, d, tm), lambda i: (i, 0, 0)),
            pl.BlockSpec((tm, mw), lambda i: (i, 0)),
        ],
        out_shape=[jax.ShapeDtypeStruct((t, d), BF16)] * 2
        + [jax.ShapeDtypeStruct((t // tm, d, tm), BF16), jax.ShapeDtypeStruct((t, mw), BF16)],
        compiler_params=pltpu.CompilerParams(
            dimension_semantics=("arbitrary",), vmem_limit_bytes=_vmem_limit(est)),
        name="in_proj",
    )(x2d, g.reshape(1, d), w)


def _conv_proj_kernel(x_ref, g_ref, w_ref, cw_ref, mix_ref, mq_ref, u_ref, *, tiles_per_seq):
    tm, d = x_ref.shape
    i = pl.program_id(0)

    @pl.when(i % tiles_per_seq == 0)
    def _():
        u_ref[0:SUBLANES, :] = jnp.zeros((SUBLANES, d), F32)

    hn = _rms(x_ref[...], g_ref[...], NORM_EPS).astype(BF16)
    gate_c = _dot(hn, w_ref[:, d:2 * d])
    h = _dot(hn, w_ref[:, 2 * d:3 * d])
    u_ref[SUBLANES:SUBLANES + tm, :] = gate_c * h
    y = cw_ref[CONV_WIDTH - 1:CONV_WIDTH, :] * u_ref[SUBLANES:SUBLANES + tm, :]
    for tap in range(CONV_WIDTH - 1):
        shift = CONV_WIDTH - 1 - tap
        y = y + cw_ref[tap:tap + 1, :] * u_ref[SUBLANES - shift:SUBLANES - shift + tm, :]
    gate_b = _dot(hn, w_ref[:, 0:d])
    mix_ref[...] = (gate_b * y).astype(BF16)
    mq_ref[...] = _dot(hn, w_ref[:, 3 * d:]).astype(BF16)
    u_ref[0:SUBLANES, :] = u_ref[tm:tm + SUBLANES, :]


def _conv_proj(x2d, g, w, conv_w, *, seq, tm):
    t, d = x2d.shape
    n = w.shape[1]
    mw = n - 3 * d
    est = 2 * tm * d * 4 + d * n * 2 + 2 * tm * (d + mw) * 2 + 5 * tm * d * 4
    return pl.pallas_call(
        functools.partial(_conv_proj_kernel, tiles_per_seq=seq // tm),
        grid=(t // tm,),
        in_specs=[
            pl.BlockSpec((tm, d), lambda i: (i, 0)),
            _const_spec((1, d)),
            _const_spec((d, n)),
            _const_spec((CONV_WIDTH, d)),
        ],
        out_specs=[
            pl.BlockSpec((tm, d), lambda i: (i, 0)),
            pl.BlockSpec((tm, mw), lambda i: (i, 0)),
        ],
        out_shape=[jax.ShapeDtypeStruct((t, d), BF16), jax.ShapeDtypeStruct((t, mw), BF16)],
        scratch_shapes=[pltpu.VMEM((tm + SUBLANES, d), F32)],
        compiler_params=pltpu.CompilerParams(
            dimension_semantics=("arbitrary",), vmem_limit_bytes=_vmem_limit(est)),
        name="conv_proj",
    )(x2d, g.reshape(1, d), w, conv_w)


def _col_reduce(x, op, reduce_fn):
    rows = x.shape[0]
    parts = [x[r * (rows // COL_REDUCE_WAYS):(r + 1) * (rows // COL_REDUCE_WAYS), :]
             for r in range(COL_REDUCE_WAYS)]
    while len(parts) > 1:
        parts = [op(parts[2 * r], parts[2 * r + 1]) for r in range(len(parts) // 2)]
    return reduce_fn(parts[0], axis=0, keepdims=True)


def _tile_validity(key_off, n_keys, qry_off, n_qry, strict):
    last_ok = key_off + n_keys - 1 < qry_off if strict else key_off + n_keys - 1 <= qry_off
    first_bad = key_off >= qry_off + n_qry - 1 if strict else key_off > qry_off + n_qry - 1
    return "all" if last_ok else ("none" if first_bad else "some")


def _sb_kernel(q_ref, k_ref, vt_ref, tri_ref, o_ref, run_ref, acc_ref, z_ref, *, tq, tk, qc, sub):
    i = pl.program_id(2)
    run_ref[...] = jnp.zeros_like(run_ref)
    acc_ref[...] = jnp.zeros_like(acc_ref)

    def load_kv(jb):
        return k_ref[0, pl.ds(pl.multiple_of(jb * tk, tk), tk), :], vt_ref[0, jb]

    def scores(c, ks):
        return jnp.minimum(_dot_nt(ks, q_ref[0, c * qc:(c + 1) * qc, :]), EXP2_ARG_MAX)

    def log_weights(c, z2, mask):
        cols = slice(c * qc, (c + 1) * qc)
        u = jnp.log2(1.0 + jnp.exp2(z2))
        if mask is not None:
            u = jnp.where(mask, u, 0.0)
        u_bf = u.astype(BF16)
        run = run_ref[:, cols]
        log_a = [None] * (tk // sub)
        for sb in reversed(range(tk // sub)):
            rows = slice(sb * sub, (sb + 1) * sub)
            suffix = _dot(tri_ref[...], u_bf[rows, :])
            log_a[sb] = z2[rows, :] - suffix - run
            run = run + suffix[0:1, :]
        run_ref[:, cols] = run
        return jnp.concatenate(log_a, axis=0)

    def accumulate(c, log_a, vt, mask):
        a = jnp.exp2(log_a)
        if mask is not None:
            a = jnp.where(mask, a, 0.0)
        acc_ref[:, c * qc:(c + 1) * qc] += _dot(vt, a.astype(BF16))

    def key_block(ks, vt, chunk_masks):
        z2 = {c: scores(c, ks) for c in chunk_masks}
        log_a = {c: log_weights(c, z2[c], chunk_masks[c]) for c in chunk_masks}
        for c in chunk_masks:
            accumulate(c, log_a[c], vt, chunk_masks[c])

    key = lax.broadcasted_iota(jnp.int32, (tk, qc), 0)
    qry = lax.broadcasted_iota(jnp.int32, (tk, qc), 1)
    for kb in reversed(range(tq // tk)):
        ks, vt = load_kv(i * (tq // tk) + kb)
        chunk_masks = {}
        for c in range(tq // qc):
            validity = _tile_validity(kb * tk, tk, c * qc, qc, strict=True)
            if validity != "none":
                chunk_masks[c] = None if validity == "all" else key + kb * tk < qry + c * qc
        key_block(ks, vt, chunk_masks)

    assert (tq // tk) % 2 == 0
    n_before = i * (tq // tk)
    chunks = range(tq // qc)

    def prefetch_scores(jb, slot):
        ks, _ = load_kv(jnp.maximum(jb, 0))
        for c in chunks:
            z_ref[slot, c] = scores(c, ks)

    def process(jb, slot):
        _, vt = load_kv(jb)
        log_a = {c: log_weights(c, z_ref[slot, c], None) for c in chunks}
        prefetch_scores(jb - 1, 1 - slot)
        for c in chunks:
            accumulate(c, log_a[c], vt, None)

    prefetch_scores(n_before - 1, 0)

    def keep_going(carry):
        jj, min_run = carry
        return jnp.logical_and(jj < n_before // 2,
                               min_run <= EXP2_ARG_MAX - F32_EXP2_UNDERFLOW_ARG)

    def body(carry):
        jj, _ = carry
        jb = n_before - 1 - 2 * jj
        process(jb, 0)
        process(jb - 1, 1)
        return jj + 1, jnp.min(run_ref[...])

    lax.while_loop(keep_going, body, (jnp.int32(0), jnp.min(run_ref[...])))
    o_ref[0] = acc_ref[...].T.astype(o_ref.dtype)


def _sb_attention(q, k, vt, *, tq):
    b, s, d = q.shape
    tk = vt.shape[3]
    heads = d // HEAD_DIM
    qc = min(tq, V7X_MXUS * V7X_MXU_DIM)
    sub = min(tk, V7X_MXU_DIM)
    r = jnp.arange(sub)
    tri = (r[None, :] >= r[:, None]).astype(BF16)
    est = 2 * s * HEAD_DIM * 2 + 12 * (tq // qc) * tk * qc * 4 + 4 * HEAD_DIM * tq * 4
    return pl.pallas_call(
        functools.partial(_sb_kernel, tq=tq, tk=tk, qc=qc, sub=sub),
        grid=(b, heads, s // tq),
        in_specs=[
            pl.BlockSpec((1, tq, HEAD_DIM), lambda bi, h, i: (bi, i, h)),
            pl.BlockSpec((1, s, HEAD_DIM), lambda bi, h, i: (bi, 0, h), pipeline_mode=pl.Buffered(1)),
            pl.BlockSpec((1, s // tk, HEAD_DIM, tk), lambda bi, h, i: (bi, 0, h, 0),
                         pipeline_mode=pl.Buffered(1)),
            _const_spec((sub, sub)),
        ],
        out_specs=pl.BlockSpec((1, tq, HEAD_DIM), lambda bi, h, i: (bi, i, h)),
        out_shape=jax.ShapeDtypeStruct((b, s, d), BF16),
        scratch_shapes=[pltpu.VMEM((1, tq), F32), pltpu.VMEM((HEAD_DIM, tq), F32),
                        pltpu.VMEM((2, tq // qc, tk, qc), F32)],
        compiler_params=pltpu.CompilerParams(
            dimension_semantics=("arbitrary", "arbitrary", "arbitrary"),
            vmem_limit_bytes=_vmem_limit(est)),
        name="sb_attention",
    )(q, k, vt, tri)


def _diff_kernel(slope2_ref, lq1_ref, lk1_ref, lq2_ref, lk2_ref, g_ref, qf_ref, kf_ref,
                 q1_ref, q2_ref, k1_ref, k2_ref, vt_ref, o_ref, m_ref, l_ref, acc_ref, s_ref,
                 *, tq, tk, qc, lambda_init):
    h = pl.program_id(1)
    i = pl.program_id(2)
    slope2 = slope2_ref[h]
    m_ref[...] = jnp.full_like(m_ref, MASK_VALUE)
    l_ref[...] = jnp.zeros_like(l_ref)
    acc_ref[...] = jnp.zeros_like(acc_ref)
    qf = jnp.broadcast_to(qf_ref[0], (qc, qf_ref.shape[2]))
    q_refs = (q1_ref, q2_ref)
    k_refs = (k1_ref, k2_ref)

    def load_kv(jb):
        start = pl.multiple_of(jb * tk, tk)
        k_aug = [jnp.concatenate([k_refs[mp][0, pl.ds(start, tk), :], kf_ref[...]], axis=1)
                 for mp in range(2)]
        return k_aug, vt_ref[0, jb]

    def scores(mp, c, k_aug):
        q_aug = jnp.concatenate([q_refs[mp][0, c * qc:(c + 1) * qc, :], qf], axis=1)
        return _dot_nt(k_aug, q_aug)

    def probabilities(mp, c, s, off, mask):
        cols = slice(c * qc, (c + 1) * qc)
        if mask is not None:
            s = jnp.where(mask, s, MASK_VALUE)
        m = m_ref[mp, :, cols]
        m_new = jnp.maximum(m, _col_reduce(s, jnp.maximum, jnp.max) + off)
        alpha = jnp.exp2(m - m_new)
        p = jnp.exp2(s - (m_new - off))
        m_ref[mp, :, cols] = m_new
        l_ref[mp, :, cols] = alpha * l_ref[mp, :, cols] + _col_reduce(p, jnp.add, jnp.sum)
        return alpha, p.astype(BF16)

    def accumulate(mp, c, alpha, p_bf, vt):
        cols = slice(c * qc, (c + 1) * qc)
        acc_ref[mp, :, cols] = alpha * acc_ref[mp, :, cols] + _dot(vt, p_bf)

    def key_block(k_aug, vt, off, chunk_masks):
        tiles = [(mp, c) for c in chunk_masks for mp in range(2)]
        s = {t: scores(*t, k_aug[t[0]]) for t in tiles[:MATMUL_LOOKAHEAD]}
        for n, t in enumerate(tiles):
            ap = probabilities(*t, s.pop(t), off, chunk_masks[t[1]])
            if n + MATMUL_LOOKAHEAD < len(tiles):
                ahead = tiles[n + MATMUL_LOOKAHEAD]
                s[ahead] = scores(*ahead, k_aug[ahead[0]])
            accumulate(*t, *ap, vt)

    key = lax.broadcasted_iota(jnp.int32, (tk, qc), 0)
    qry = lax.broadcasted_iota(jnp.int32, (tk, qc), 1)
    for kb in reversed(range(tq // tk)):
        k_aug, vt = load_kv(i * (tq // tk) + kb)
        chunk_masks = {}
        for c in range(tq // qc):
            validity = _tile_validity(kb * tk, tk, c * qc, qc, strict=False)
            if validity != "none":
                chunk_masks[c] = None if validity == "all" else key + kb * tk <= qry + c * qc
        key_block(k_aug, vt, slope2 * (kb * tk), chunk_masks)

    assert (tq // tk) % 2 == 0
    n_before = i * (tq // tk)
    tiles = [(mp, c) for c in range(tq // qc) for mp in range(2)]

    def prefetch_scores(jb, slot):
        k_aug, _ = load_kv(jnp.maximum(jb, 0))
        for n, t in enumerate(tiles):
            s_ref[slot, n] = scores(*t, k_aug[t[0]])

    def process(jb, slot):
        _, vt = load_kv(jb)
        off = slope2 * (jb * tk - i * tq).astype(F32)
        ap = [probabilities(*t, s_ref[slot, n], off, None) for n, t in enumerate(tiles)]
        prefetch_scores(jb - 1, 1 - slot)
        for n, t in enumerate(tiles):
            accumulate(*t, *ap[n], vt)

    prefetch_scores(n_before - 1, 0)

    def body(jj, carry):
        jb = n_before - 1 - 2 * jj
        process(jb, 0)
        process(jb - 1, 1)
        return carry

    lax.fori_loop(0, n_before // 2, body, 0)

    lam =(jnp.exp(jnp.sum(lq1_ref[...] * lk1_ref[...], axis=1, keepdims=True))
           - jnp.exp(jnp.sum(lq2_ref[...] * lk2_ref[...], axis=1, keepdims=True))
           + lambda_init)
    o_t = acc_ref[0] / l_ref[0] - lam * (acc_ref[1] / l_ref[1])
    ms = jnp.mean(o_t * o_t, axis=0, keepdims=True)
    o_t = o_t * lax.rsqrt(ms + HEAD_NORM_EPS) * (g_ref[...] * (1.0 - lambda_init))
    o_ref[0] = o_t.T.astype(o_ref.dtype)


def _alibi_features(slope2, blk):
    pieces = []
    rest = slope2
    for _ in range(ALIBI_SLOPE_PIECES):
        piece = rest.astype(BF16)
        pieces.append(piece)
        rest = rest - piece.astype(F32)
    qf = jnp.stack([p * r for p in pieces for r in (ALIBI_POS_RADIX, 1)], axis=1)
    pos = jnp.arange(blk)
    digits = jnp.stack([pos // ALIBI_POS_RADIX, pos % ALIBI_POS_RADIX] * ALIBI_SLOPE_PIECES, axis=1)
    pad = HEAD_DIM - 2 * ALIBI_SLOPE_PIECES
    qf = jnp.pad(qf.astype(BF16), ((0, 0), (0, pad)))[:, None, :]
    kf = jnp.pad(digits.astype(BF16), ((0, 0), (0, pad)))
    return qf, kf


def _diff_attention(q, k, vt, lq1, lk1, lq2, lk2, g_head, *, layer_idx, tq):
    b, s, d = q.shape
    tk = vt.shape[3]
    heads = d // DIFF_V_DIM
    qc = min(tq, V7X_MXUS * V7X_MXU_DIM)
    assert tk <= ALIBI_POS_RADIX * 256
    lambda_init = 0.8 - 0.6 * math.exp(-0.3 * layer_idx)
    slope2 = LOG2_E * 2.0 ** (-8.0 * jnp.arange(1, heads + 1, dtype=F32) / heads)
    qf, kf = _alibi_features(slope2, tk)
    vec = lambda a: a.reshape(1, -1).astype(F32)
    est = (s * (2 * HEAD_DIM + DIFF_V_DIM) * 2 + 12 * 2 * (tq // qc) * tk * qc * 4
           + 4 * tq * DIFF_V_DIM * 4)
    return pl.pallas_call(
        functools.partial(_diff_kernel, tq=tq, tk=tk, qc=qc, lambda_init=lambda_init),
        grid=(b, heads, s // tq),
        in_specs=[
            pl.BlockSpec(memory_space=pltpu.SMEM),
            _const_spec((1, HEAD_DIM)), _const_spec((1, HEAD_DIM)),
            _const_spec((1, HEAD_DIM)), _const_spec((1, HEAD_DIM)),
            _const_spec((DIFF_V_DIM, 1)),
            pl.BlockSpec((1, 1, HEAD_DIM), lambda bi, h, i: (h, 0, 0)),
            _const_spec((tk, HEAD_DIM)),
            pl.BlockSpec((1, tq, HEAD_DIM), lambda bi, h, i: (bi, i, 2 * h)),
            pl.BlockSpec((1, tq, HEAD_DIM), lambda bi, h, i: (bi, i, 2 * h + 1)),
            pl.BlockSpec((1, s, HEAD_DIM), lambda bi, h, i: (bi, 0, 2 * h),
                         pipeline_mode=pl.Buffered(1)),
            pl.BlockSpec((1, s, HEAD_DIM), lambda bi, h, i: (bi, 0, 2 * h + 1),
                         pipeline_mode=pl.Buffered(1)),
            pl.BlockSpec((1, s // tk, DIFF_V_DIM, tk), lambda bi, h, i: (bi, 0, h, 0),
                         pipeline_mode=pl.Buffered(1)),
        ],
        out_specs=pl.BlockSpec((1, tq, DIFF_V_DIM), lambda bi, h, i: (bi, i, h)),
        out_shape=jax.ShapeDtypeStruct((b, s, d), BF16),
        scratch_shapes=[pltpu.VMEM((2, 1, tq), F32), pltpu.VMEM((2, 1, tq), F32),
                        pltpu.VMEM((2, DIFF_V_DIM, tq), F32),
                        pltpu.VMEM((2, 2 * (tq // qc), tk, qc), F32)],
        compiler_params=pltpu.CompilerParams(
            dimension_semantics=("arbitrary", "arbitrary", "arbitrary"),
            vmem_limit_bytes=_vmem_limit(est)),
        name="diff_attention",
    )(slope2, vec(lq1), vec(lk1), vec(lq2), vec(lk2), g_head.reshape(-1, 1).astype(F32), qf, kf,
      q, q, k, k, vt)


def _post_kernel(mix_ref, mq_ref, x_ref, km_ref, vm_ref, wo_ref, g_ref, w1_ref, w2_ref, gf_ref,
                 o_ref, *, ff_chunks, final_norm):
    tm, d = x_ref.shape
    mw = mq_ref.shape[1]
    d_ff = w2_ref.shape[0]
    head_dim = mw // MEM_HEADS

    mq = mq_ref[...]
    km = km_ref[0]
    vm = vm_ref[0]
    lane_head = lax.broadcasted_iota(jnp.int32, (tm, mw), 1) // head_dim
    mo = jnp.zeros((tm, mw), F32)
    for hd in range(MEM_HEADS):
        in_head = lane_head == hd
        s = _dot_nt(jnp.where(in_head, mq, jnp.zeros_like(mq)), km)
        p = jnp.exp(s - jnp.max(s, axis=1, keepdims=True))
        l = jnp.sum(p, axis=1, keepdims=True)
        mo = jnp.where(in_head, _dot(p.astype(BF16), vm) / l, mo)

    y = x_ref[...] + _dot(mix_ref[...], wo_ref[0:d, :]) + _dot(mo.astype(BF16), wo_ref[d:, :])

    yn = _rms(y, g_ref[...], NORM_EPS).astype(BF16)
    hidden = []
    for lo, hi in ff_chunks:
        gate = _dot(yn, w1_ref[:, lo:hi])
        up = _dot(yn, w1_ref[:, d_ff + lo:d_ff + hi])
        hidden.append((gate * (1.0 / (1.0 + jnp.exp(-gate))) * up).astype(BF16))
    y = y + _dot(jnp.concatenate(hidden, axis=1), w2_ref[...])
    if final_norm:
        y = _rms(y, gf_ref[...], NORM_EPS)
    o_ref[...] = y


def _ff_chunks(d_ff, n_chunks, align):
    tiles = d_ff // align
    bounds = [align * ((tiles * c) // n_chunks) for c in range(n_chunks)] + [d_ff]
    return tuple((bounds[c], bounds[c + 1]) for c in range(n_chunks))


def _post(mix2d, mq2d, x2d, km, vm, wo, g_ffn, w1, w2, g_final, *, seq, tm, final_norm):
    t, d = x2d.shape
    mw = mq2d.shape[1]
    mem_len = km.shape[1]
    d_ff = w2.shape[0]
    tiles_per_seq = seq // tm
    ff_chunks = _ff_chunks(d_ff, 2, 256)
    chunk = max(hi - lo for lo, hi in ff_chunks)
    est = ((wo.size + w1.size + w2.size) * 2 + 2 * tm * (d + mw) * 2 + 4 * tm * d * 4
           + tm * chunk * 12 + 4 * tm * d * 4)
    return pl.pallas_call(
        functools.partial(_post_kernel, ff_chunks=ff_chunks, final_norm=final_norm),
        grid=(t // tm,),
        in_specs=[
            pl.BlockSpec((tm, d), lambda i: (i, 0)),
            pl.BlockSpec((tm, mw), lambda i: (i, 0)),
            pl.BlockSpec((tm, d), lambda i: (i, 0)),
            pl.BlockSpec((1, mem_len, mw), lambda i: (i // tiles_per_seq, 0, 0)),
            pl.BlockSpec((1, mem_len, mw), lambda i: (i // tiles_per_seq, 0, 0)),
            _const_spec(wo.shape),
            _const_spec((1, d)),
            _const_spec(w1.shape),
            _const_spec(w2.shape),
            _const_spec((1, d)),
        ],
        out_specs=pl.BlockSpec((tm, d), lambda i: (i, 0)),
        out_shape=jax.ShapeDtypeStruct((t, d), F32),
        compiler_params=pltpu.CompilerParams(
            dimension_semantics=("arbitrary",), vmem_limit_bytes=_vmem_limit(est)),
        name="post",
    )(mix2d, mq2d, x2d, km, vm, wo, g_ffn.reshape(1, d), w1, w2, g_final.reshape(1, d))


def kernel(x, mem, g_mix, w_in, w_mem_kv, w_o, g_ffn, w_ffn_in, w_ffn_out,
           lam_q1, lam_k1, lam_q2, lam_k2, g_diff_head, conv_w, g_mem, g_final):
    b, seq, d = x.shape
    mem_len = mem.shape[1]
    depth = w_in.shape[0]
    t = b * seq
    tm = min(512, seq)
    sb_tq = min(1024, seq)
    diff_tq = min(1024, seq)

    km_all, vm_all = _mem_kv(mem.reshape(b * mem_len, d), g_mem, w_mem_kv.astype(BF16))
    mw = km_all.shape[-1]
    km_all = km_all.reshape(depth, b, mem_len, mw)
    vm_all = vm_all.reshape(depth, b, mem_len, mw)

    x2d = x.reshape(t, d)
    for i in range(depth):
        kind = i % N_MIXERS
        j = i // N_MIXERS
        w = w_in[i].astype(BF16)
        if kind == 2:
            mix, mq = _conv_proj(x2d, g_mix[i], w, conv_w[j], seq=seq, tm=tm)
        else:
            q, k, vt, mq = _in_proj(x2d, g_mix[i], w, q_scale=LOG2_E * HEAD_DIM ** -0.5, tm=tm)
            q, k = (a.reshape(b, seq, d) for a in (q, k))
            vt = vt.reshape(b, seq // tm, d, tm)
            if kind == 0:
                mix = _sb_attention(q, k, vt, tq=sb_tq)
            else:
                mix = _diff_attention(q, k, vt, lam_q1[j], lam_k1[j], lam_q2[j], lam_k2[j],
                                      g_diff_head[j], layer_idx=i, tq=diff_tq)
            mix = mix.reshape(t, d)
        x2d = _post(mix, mq, x2d, km_all[i], vm_all[i], w_o[i].astype(BF16), g_ffn[i],
                    w_ffn_in[i].astype(BF16), w_ffn_out[i].astype(BF16), g_final,
                    seq=seq, tm=tm, final_norm=(i == depth - 1))
    return x2d.reshape(b, seq, d)
```

```python
import functools
import math

import jax
import jax.numpy as jnp
from jax import lax
from jax.experimental import pallas as pl
from jax.experimental.pallas import tpu as pltpu

F32 = jnp.float32
BF16 = jnp.bfloat16

N_MIXERS = 3
HEAD_DIM = 128
DIFF_V_DIM = 2 * HEAD_DIM
MEM_HEADS = 4
CONV_WIDTH = 3
NORM_EPS = 1e-6
HEAD_NORM_EPS = 1e-5
MASK_VALUE = -1e30
EXP2_ARG_MAX = 126.0
F32_EXP2_UNDERFLOW_ARG = -150.0
NORM_BOUND_MARGIN = 1.0 + 2.0 ** -6
SCORE_BOUND_SLACK = 2.0
ALIBI_SLOPE_PIECES = 3
ALIBI_POS_RADIX = 16

V7X_VMEM_BYTES = 64 * 1024 * 1024
V7X_MXU_DIM = 256
V7X_MXUS = 2
LOG2_E = 1.4426950408889634
SUBLANES = 8
COL_REDUCE_WAYS = 8
MATMUL_LOOKAHEAD = 2


def _vmem_limit(estimate_bytes):
    return int(min(max(estimate_bytes, 16 * 1024 * 1024), V7X_VMEM_BYTES - 8 * 1024 * 1024))


def _rms(xf, g, eps):
    ms = jnp.mean(xf * xf, axis=-1, keepdims=True)
    return xf * lax.rsqrt(ms + eps) * g


def _dot(a, b):
    return jnp.dot(a, b, preferred_element_type=F32)


def _dot_nt(a, b):
    return lax.dot_general(a, b, (((1,), (1,)), ((), ())), preferred_element_type=F32)


def _const_spec(shape):
    nd = len(shape)
    return pl.BlockSpec(shape, lambda *_: (0,) * nd, pipeline_mode=pl.Buffered(1))


def _mem_kv_kernel(mem_ref, g_ref, w_ref, k_ref, v_ref, *, mem_width):
    mem_n = _rms(mem_ref[...], g_ref[...], NORM_EPS).astype(BF16)
    kv = _dot(mem_n, w_ref[0])
    head_dim = mem_width // MEM_HEADS
    k_ref[0] = (kv[:, :mem_width] * head_dim ** -0.5).astype(BF16)
    v_ref[0] = kv[:, mem_width:].astype(BF16)


def _mem_kv(mem2d, g_mem, w_kv):
    depth, d_model, two_w = w_kv.shape
    mem_width = two_w // 2
    rows = mem2d.shape[0]
    return pl.pallas_call(
        functools.partial(_mem_kv_kernel, mem_width=mem_width),
        grid=(depth,),
        in_specs=[
            pl.BlockSpec((rows, d_model), lambda i: (0, 0)),
            pl.BlockSpec((1, d_model), lambda i: (0, 0)),
            pl.BlockSpec((1, d_model, two_w), lambda i: (i, 0, 0)),
        ],
        out_specs=[
            pl.BlockSpec((1, rows, mem_width), lambda i: (i, 0, 0)),
            pl.BlockSpec((1, rows, mem_width), lambda i: (i, 0, 0)),
        ],
        out_shape=[jax.ShapeDtypeStruct((depth, rows, mem_width), BF16)] * 2,
        name="mem_kv",
    )(mem2d, g_mem.reshape(1, d_model), w_kv)


def _in_proj_kernel(x_ref, g_ref, w_ref, q_ref, k_ref, vt_ref, mq_ref, *, q_scale):
    d = x_ref.shape[1]
    hn = _rms(x_ref[...], g_ref[...], NORM_EPS).astype(BF16)
    q_ref[...] = (_dot(hn, w_ref[:, 0:d]) * q_scale).astype(BF16)
    k_ref[...] = _dot(hn, w_ref[:, d:2 * d]).astype(BF16)
    vt_ref[0] = _dot(hn, w_ref[:, 2 * d:3 * d]).T.astype(BF16)
    mq_ref[...] = _dot(hn, w_ref[:, 3 * d:]).astype(BF16)


def _in_proj(x2d, g, w, *, q_scale, tm):
    t, d = x2d.shape
    n = w.shape[1]
    mw = n - 3 * d
    est = 2 * tm * d * 4 + d * n * 2 + 2 * tm * n * 2 + 3 * tm * d * 4
    return pl.pallas_call(
        functools.partial(_in_proj_kernel, q_scale=q_scale),
        grid=(t // tm,),
        in_specs=[
            pl.BlockSpec((tm, d), lambda i: (i, 0)),
            _const_spec((1, d)),
            _const_spec((d, n)),
        ],
        out_specs=[
            pl.BlockSpec((tm, d), lambda i: (i, 0)),
            pl.BlockSpec((tm, d), lambda i: (i, 0)),
            pl.BlockSpec((1, d, tm), lambda i: (i, 0, 0)),
            pl.BlockSpec((tm, mw), lambda i: (i, 0)),
        ],
        out_shape=[jax.ShapeDtypeStruct((t, d), BF16)] * 2
        + [jax.ShapeDtypeStruct((t // tm, d, tm), BF16), jax.ShapeDtypeStruct((t, mw), BF16)],
        compiler_params=pltpu.CompilerParams(
            dimension_semantics=("arbitrary",), vmem_limit_bytes=_vmem_limit(est)),
        name="in_proj",
    )(x2d, g.reshape(1, d), w)


def _conv_proj_kernel(x_ref, g_ref, w_ref, cw_ref, mix_ref, mq_ref, u_ref, *, tiles_per_seq):
    tm, d = x_ref.shape
    i = pl.program_id(0)

    @pl.when(i % tiles_per_seq == 0)
    def _():
        u_ref[0:SUBLANES, :] = jnp.zeros((SUBLANES, d), F32)

    hn = _rms(x_ref[...], g_ref[...], NORM_EPS).astype(BF16)
    gate_c = _dot(hn, w_ref[:, d:2 * d])
    h = _dot(hn, w_ref[:, 2 * d:3 * d])
    u_ref[SUBLANES:SUBLANES + tm, :] = gate_c * h
    y = cw_ref[CONV_WIDTH - 1:CONV_WIDTH, :] * u_ref[SUBLANES:SUBLANES + tm, :]
    for tap in range(CONV_WIDTH - 1):
        shift = CONV_WIDTH - 1 - tap
        y = y + cw_ref[tap:tap + 1, :] * u_ref[SUBLANES - shift:SUBLANES - shift + tm, :]
    gate_b = _dot(hn, w_ref[:, 0:d])
    mix_ref[...] = (gate_b * y).astype(BF16)
    mq_ref[...] = _dot(hn, w_ref[:, 3 * d:]).astype(BF16)
    u_ref[0:SUBLANES, :] = u_ref[tm:tm + SUBLANES, :]


def _conv_proj(x2d, g, w, conv_w, *, seq, tm):
    t, d = x2d.shape
    n = w.shape[1]
    mw = n - 3 * d
    est = 2 * tm * d * 4 + d * n * 2 + 2 * tm * (d + mw) * 2 + 5 * tm * d * 4
    return pl.pallas_call(
        functools.partial(_conv_proj_kernel, tiles_per_seq=seq // tm),
        grid=(t // tm,),
        in_specs=[
            pl.BlockSpec((tm, d), lambda i: (i, 0)),
            _const_spec((1, d)),
            _const_spec((d, n)),
            _const_spec((CONV_WIDTH, d)),
        ],
        out_specs=[
            pl.BlockSpec((tm, d), lambda i: (i, 0)),
            pl.BlockSpec((tm, mw), lambda i: (i, 0)),
        ],
        out_shape=[jax.ShapeDtypeStruct((t, d), BF16), jax.ShapeDtypeStruct((t, mw), BF16)],
        scratch_shapes=[pltpu.VMEM((tm + SUBLANES, d), F32)],
        compiler_params=pltpu.CompilerParams(
            dimension_semantics=("arbitrary",), vmem_limit_bytes=_vmem_limit(est)),
        name="conv_proj",
    )(x2d, g.reshape(1, d), w, conv_w)


def _col_reduce(x, op, reduce_fn):
    rows = x.shape[0]
    parts = [x[r * (rows // COL_REDUCE_WAYS):(r + 1) * (rows // COL_REDUCE_WAYS), :]
             for r in range(COL_REDUCE_WAYS)]
    while len(parts) > 1:
        parts = [op(parts[2 * r], parts[2 * r + 1]) for r in range(len(parts) // 2)]
    return reduce_fn(parts[0], axis=0, keepdims=True)


def _tile_validity(key_off, n_keys, qry_off, n_qry, strict):
    last_ok = key_off + n_keys - 1 < qry_off if strict else key_off + n_keys - 1 <= qry_off
    first_bad = key_off >= qry_off + n_qry - 1 if strict else key_off > qry_off + n_qry - 1
    return "all" if last_ok else ("none" if first_bad else "some")


def _sb_kernel(q_ref, k_ref, vt_ref, tri_ref, o_ref, run_ref, acc_ref, z_ref, *, tq, tk, qc, sub):
    i = pl.program_id(2)
    run_ref[...] = jnp.zeros_like(run_ref)
    acc_ref[...] = jnp.zeros_like(acc_ref)

    def load_kv(jb):
        return k_ref[0, pl.ds(pl.multiple_of(jb * tk, tk), tk), :], vt_ref[0, jb]

    def scores(c, ks):
        return jnp.minimum(_dot_nt(ks, q_ref[0, c * qc:(c + 1) * qc, :]), EXP2_ARG_MAX)

    def log_weights(c, z2, mask):
        cols = slice(c * qc, (c + 1) * qc)
        u = jnp.log2(1.0 + jnp.exp2(z2))
        if mask is not None:
            u = jnp.where(mask, u, 0.0)
        u_bf = u.astype(BF16)
        run = run_ref[:, cols]
        log_a = [None] * (tk // sub)
        for sb in reversed(range(tk // sub)):
            rows = slice(sb * sub, (sb + 1) * sub)
            suffix = _dot(tri_ref[...], u_bf[rows, :])
            log_a[sb] = z2[rows, :] - suffix - run
            run = run + suffix[0:1, :]
        run_ref[:, cols] = run
        return jnp.concatenate(log_a, axis=0)

    def accumulate(c, log_a, vt, mask):
        a = jnp.exp2(log_a)
        if mask is not None:
            a = jnp.where(mask, a, 0.0)
        acc_ref[:, c * qc:(c + 1) * qc] += _dot(vt, a.astype(BF16))

    def key_block(ks, vt, chunk_masks):
        z2 = {c: scores(c, ks) for c in chunk_masks}
        log_a = {c: log_weights(c, z2[c], chunk_masks[c]) for c in chunk_masks}
        for c in chunk_masks:
            accumulate(c, log_a[c], vt, chunk_masks[c])

    key = lax.broadcasted_iota(jnp.int32, (tk, qc), 0)
    qry = lax.broadcasted_iota(jnp.int32, (tk, qc), 1)
    for kb in reversed(range(tq // tk)):
        ks, vt = load_kv(i * (tq // tk) + kb)
        chunk_masks = {}
        for c in range(tq // qc):
            validity = _tile_validity(kb * tk, tk, c * qc, qc, strict=True)
            if validity != "none":
                chunk_masks[c] = None if validity == "all" else key + kb * tk < qry + c * qc
        key_block(ks, vt, chunk_masks)

    assert (tq // tk) % 2 == 0
    n_before = i * (tq // tk)
    chunks = range(tq // qc)

    def prefetch_scores(jb, slot):
        ks, _ = load_kv(jnp.maximum(jb, 0))
        for c in chunks:
            z_ref[slot, c] = scores(c, ks)

    def process(jb, slot):
        _, vt = load_kv(jb)
        log_a = {c: log_weights(c, z_ref[slot, c], None) for c in chunks}
        prefetch_scores(jb - 1, 1 - slot)
        for c in chunks:
            accumulate(c, log_a[c], vt, None)

    prefetch_scores(n_before - 1, 0)

    def weights_remain(min_run):
        return min_run <= EXP2_ARG_MAX - F32_EXP2_UNDERFLOW_ARG

    def keep_going(carry):
        jj, min_run = carry
        return jnp.logical_and(jj < n_before // 2, weights_remain(min_run))

    def body(carry):
        jj, _ = carry
        jb = n_before - 1 - 2 * jj
        process(jb, 0)

        @pl.when(weights_remain(jnp.min(run_ref[...])))
        def _():
            process(jb - 1, 1)

        return jj + 1, jnp.min(run_ref[...])

    lax.while_loop(keep_going, body, (jnp.int32(0), jnp.min(run_ref[...])))
    o_ref[0] = acc_ref[...].T.astype(o_ref.dtype)


def _sb_attention(q, k, vt, *, tq):
    b, s, d = q.shape
    tk = vt.shape[3]
    heads = d // HEAD_DIM
    qc = min(tq, V7X_MXUS * V7X_MXU_DIM)
    sub = min(tk, V7X_MXU_DIM)
    r = jnp.arange(sub)
    tri = (r[None, :] >= r[:, None]).astype(BF16)
    est = 2 * s * HEAD_DIM * 2 + 12 * (tq // qc) * tk * qc * 4 + 4 * HEAD_DIM * tq * 4
    return pl.pallas_call(
        functools.partial(_sb_kernel, tq=tq, tk=tk, qc=qc, sub=sub),
        grid=(b, heads, s // tq),
        in_specs=[
            pl.BlockSpec((1, tq, HEAD_DIM), lambda bi, h, i: (bi, i, h)),
            pl.BlockSpec((1, s, HEAD_DIM), lambda bi, h, i: (bi, 0, h), pipeline_mode=pl.Buffered(1)),
            pl.BlockSpec((1, s // tk, HEAD_DIM, tk), lambda bi, h, i: (bi, 0, h, 0),
                         pipeline_mode=pl.Buffered(1)),
            _const_spec((sub, sub)),
        ],
        out_specs=pl.BlockSpec((1, tq, HEAD_DIM), lambda bi, h, i: (bi, i, h)),
        out_shape=jax.ShapeDtypeStruct((b, s, d), BF16),
        scratch_shapes=[pltpu.VMEM((1, tq), F32), pltpu.VMEM((HEAD_DIM, tq), F32),
                        pltpu.VMEM((2, tq // qc, tk, qc), F32)],
        compiler_params=pltpu.CompilerParams(
            dimension_semantics=("arbitrary", "arbitrary", "arbitrary"),
            vmem_limit_bytes=_vmem_limit(est)),
        name="sb_attention",
    )(q, k, vt, tri)


def _diff_kernel(slope2_ref, lq1_ref, lk1_ref, lq2_ref, lk2_ref, g_ref, qf_ref, kf_ref,
                 q1_ref, q2_ref, k1_ref, k2_ref, vt_ref, o_ref, m_ref, l_ref, acc_ref, s_ref,
                 kmax2_ref,
                 *, tq, tk, qc, lambda_init):
    h = pl.program_id(1)
    i = pl.program_id(2)
    slope2 = slope2_ref[h]
    m_ref[...] = jnp.full_like(m_ref, MASK_VALUE)
    l_ref[...] = jnp.zeros_like(l_ref)
    acc_ref[...] = jnp.zeros_like(acc_ref)
    qf = jnp.broadcast_to(qf_ref[0], (qc, qf_ref.shape[2]))
    q_refs = (q1_ref, q2_ref)
    k_refs = (k1_ref, k2_ref)

    def load_kv(jb):
        start = pl.multiple_of(jb * tk, tk)
        k_aug = [jnp.concatenate([k_refs[mp][0, pl.ds(start, tk), :], kf_ref[...]], axis=1)
                 for mp in range(2)]
        return k_aug, vt_ref[0, jb]

    def scores(mp, c, k_aug):
        q_aug = jnp.concatenate([q_refs[mp][0, c * qc:(c + 1) * qc, :], qf], axis=1)
        return _dot_nt(k_aug, q_aug)

    def probabilities(mp, c, s, off, mask):
        cols = slice(c * qc, (c + 1) * qc)
        if mask is not None:
            s = jnp.where(mask, s, MASK_VALUE)
        m = m_ref[mp, :, cols]
        m_new = jnp.maximum(m, _col_reduce(s, jnp.maximum, jnp.max) + off)
        alpha = jnp.exp2(m - m_new)
        p = jnp.exp2(s - (m_new - off))
        m_ref[mp, :, cols] = m_new
        l_ref[mp, :, cols] = alpha * l_ref[mp, :, cols] + _col_reduce(p, jnp.add, jnp.sum)
        return alpha, p.astype(BF16)

    def accumulate(mp, c, alpha, p_bf, vt):
        cols = slice(c * qc, (c + 1) * qc)
        acc_ref[mp, :, cols] = alpha * acc_ref[mp, :, cols] + _dot(vt, p_bf)

    def key_block(k_aug, vt, off, chunk_masks):
        tiles = [(mp, c) for c in chunk_masks for mp in range(2)]
        s = {t: scores(*t, k_aug[t[0]]) for t in tiles[:MATMUL_LOOKAHEAD]}
        for n, t in enumerate(tiles):
            ap = probabilities(*t, s.pop(t), off, chunk_masks[t[1]])
            if n + MATMUL_LOOKAHEAD < len(tiles):
                ahead = tiles[n + MATMUL_LOOKAHEAD]
                s[ahead] = scores(*ahead, k_aug[ahead[0]])
            accumulate(*t, *ap, vt)

    key = lax.broadcasted_iota(jnp.int32, (tk, qc), 0)
    qry = lax.broadcasted_iota(jnp.int32, (tk, qc), 1)
    for kb in reversed(range(tq // tk)):
        k_aug, vt = load_kv(i * (tq // tk) + kb)
        chunk_masks = {}
        for c in range(tq // qc):
            validity = _tile_validity(kb * tk, tk, c * qc, qc, strict=False)
            if validity != "none":
                chunk_masks[c] = None if validity == "all" else key + kb * tk <= qry + c * qc
        key_block(k_aug, vt, slope2 * (kb * tk), chunk_masks)

    assert (tq // tk) % 2 == 0
    n_before = i * (tq // tk)
    tiles = [(mp, c) for c in range(tq // qc) for mp in range(2)]

    def prefetch_scores(jb, slot):
        k_aug, _ = load_kv(jnp.maximum(jb, 0))
        for n, t in enumerate(tiles):
            s_ref[slot, n] = scores(*t, k_aug[t[0]])

    def process(jb, slot):
        _, vt = load_kv(jb)
        off = slope2 * (jb * tk - i * tq).astype(F32)
        ap = [probabilities(*t, s_ref[slot, n], off, None) for n, t in enumerate(tiles)]
        prefetch_scores(jb - 1, 1 - slot)
        for n, t in enumerate(tiles):
            accumulate(*t, *ap[n], vt)

    prefetch_scores(n_before - 1, 0)

    @pl.when(i == 0)
    def _():
        for mp in range(2):
            def norm_step(j, best):
                ks = k_refs[mp][0, pl.ds(pl.multiple_of(j * tk, tk), tk), :].astype(F32)
                best = jnp.maximum(best, jnp.max(jnp.sum(ks * ks, axis=1, keepdims=True)))
                kmax2_ref[mp, j] = best
                return best
            lax.fori_loop(0, k1_ref.shape[1] // tk, norm_step, jnp.float32(0.0))

    ones = jnp.ones((SUBLANES, q1_ref.shape[2]), BF16)
    qnorm2 = []
    for mp in range(2):
        q32 = q_refs[mp][0].astype(F32)
        qnorm2.append(_dot_nt(ones, (q32 * q32).astype(BF16))[0:1, :] * NORM_BOUND_MARGIN)

    def score_gap(jb):
        jb = jnp.maximum(jb, 0)
        gap = [jnp.max(jnp.sqrt(qnorm2[mp] * kmax2_ref[mp, jb]) - m_ref[mp]) for mp in range(2)]
        return (jnp.maximum(gap[0], gap[1]) + slope2 * (tk - 1)
                + slope2 * (jb * tk - i * tq).astype(F32))

    def keep_going(carry):
        jj, gap = carry
        return jnp.logical_and(jj < n_before // 2, gap >= F32_EXP2_UNDERFLOW_ARG - SCORE_BOUND_SLACK)

    def body(carry):
        jj, _ = carry
        jb = n_before - 1 - 2 * jj
        process(jb, 0)
        process(jb - 1, 1)
        return jj + 1, score_gap(jb - 2)

    lax.while_loop(keep_going, body, (jnp.int32(0), score_gap(n_before - 1)))

    lam = (jnp.exp(jnp.sum(lq1_ref[...] * lk1_ref[...], axis=1, keepdims=True))
           - jnp.exp(jnp.sum(lq2_ref[...] * lk2_ref[...], axis=1, keepdims=True))
           + lambda_init)
    o_t = acc_ref[0] / l_ref[0] - lam * (acc_ref[1] / l_ref[1])
    ms = jnp.mean(o_t * o_t, axis=0, keepdims=True)
    o_t = o_t * lax.rsqrt(ms + HEAD_NORM_EPS) * (g_ref[...] * (1.0 - lambda_init))
    o_ref[0] = o_t.T.astype(o_ref.dtype)


def _alibi_features(slope2, blk):
    pieces = []
    rest = slope2
    for _ in range(ALIBI_SLOPE_PIECES):
        piece = rest.astype(BF16)
        pieces.append(piece)
        rest = rest - piece.astype(F32)
    qf = jnp.stack([p * r for p in pieces for r in (ALIBI_POS_RADIX, 1)], axis=1)
    pos = jnp.arange(blk)
    digits = jnp.stack([pos // ALIBI_POS_RADIX, pos % ALIBI_POS_RADIX] * ALIBI_SLOPE_PIECES, axis=1)
    pad = HEAD_DIM - 2 * ALIBI_SLOPE_PIECES
    qf = jnp.pad(qf.astype(BF16), ((0, 0), (0, pad)))[:, None, :]
    kf = jnp.pad(digits.astype(BF16), ((0, 0), (0, pad)))
    return qf, kf


def _diff_attention(q, k, vt, lq1, lk1, lq2, lk2, g_head, *, layer_idx, tq):
    b, s, d = q.shape
    tk = vt.shape[3]
    heads = d // DIFF_V_DIM
    qc = min(tq, V7X_MXUS * V7X_MXU_DIM)
    assert tk <= ALIBI_POS_RADIX * 256
    lambda_init = 0.8 - 0.6 * math.exp(-0.3 * layer_idx)
    slope2 = LOG2_E * 2.0 ** (-8.0 * jnp.arange(1, heads + 1, dtype=F32) / heads)
    qf, kf = _alibi_features(slope2, tk)
    vec = lambda a: a.reshape(1, -1).astype(F32)
    est = (s * (2 * HEAD_DIM + DIFF_V_DIM) * 2 + 12 * 2 * (tq // qc) * tk * qc * 4
           + 4 * tq * DIFF_V_DIM * 4)
    return pl.pallas_call(
        functools.partial(_diff_kernel, tq=tq, tk=tk, qc=qc, lambda_init=lambda_init),
        grid=(b, heads, s // tq),
        in_specs=[
            pl.BlockSpec(memory_space=pltpu.SMEM),
            _const_spec((1, HEAD_DIM)), _const_spec((1, HEAD_DIM)),
            _const_spec((1, HEAD_DIM)), _const_spec((1, HEAD_DIM)),
            _const_spec((DIFF_V_DIM, 1)),
            pl.BlockSpec((1, 1, HEAD_DIM), lambda bi, h, i: (h, 0, 0)),
            _const_spec((tk, HEAD_DIM)),
            pl.BlockSpec((1, tq, HEAD_DIM), lambda bi, h, i: (bi, i, 2 * h)),
            pl.BlockSpec((1, tq, HEAD_DIM), lambda bi, h, i: (bi, i, 2 * h + 1)),
            pl.BlockSpec((1, s, HEAD_DIM), lambda bi, h, i: (bi, 0, 2 * h),
                         pipeline_mode=pl.Buffered(1)),
            pl.BlockSpec((1, s, HEAD_DIM), lambda bi, h, i: (bi, 0, 2 * h + 1),
                         pipeline_mode=pl.Buffered(1)),
            pl.BlockSpec((1, s // tk, DIFF_V_DIM, tk), lambda bi, h, i: (bi, 0, h, 0),
                         pipeline_mode=pl.Buffered(1)),
        ],
        out_specs=pl.BlockSpec((1, tq, DIFF_V_DIM), lambda bi, h, i: (bi, i, h)),
        out_shape=jax.ShapeDtypeStruct((b, s, d), BF16),
        scratch_shapes=[pltpu.VMEM((2, 1, tq), F32), pltpu.VMEM((2, 1, tq), F32),
                        pltpu.VMEM((2, DIFF_V_DIM, tq), F32),
                        pltpu.VMEM((2, 2 * (tq // qc), tk, qc), F32),
                        pltpu.SMEM((2, s // tk), F32)],
        compiler_params=pltpu.CompilerParams(
            dimension_semantics=("arbitrary", "arbitrary", "arbitrary"),
            vmem_limit_bytes=_vmem_limit(est)),
        name="diff_attention",
    )(slope2, vec(lq1), vec(lk1), vec(lq2), vec(lk2), g_head.reshape(-1, 1).astype(F32), qf, kf,
      q, q, k, k, vt)


def _post_kernel(mix_ref, mq_ref, x_ref, km_ref, vm_ref, wo_ref, g_ref, w1_ref, w2_ref, gf_ref,
                 o_ref, *, ff_chunks, final_norm):
    tm, d = x_ref.shape
    mw = mq_ref.shape[1]
    d_ff = w2_ref.shape[0]
    head_dim = mw // MEM_HEADS

    mq = mq_ref[...]
    km = km_ref[0]
    vm = vm_ref[0]
    lane_head = lax.broadcasted_iota(jnp.int32, (tm, mw), 1) // head_dim
    mo = jnp.zeros((tm, mw), F32)
    for hd in range(MEM_HEADS):
        in_head = lane_head == hd
        s = _dot_nt(jnp.where(in_head, mq, jnp.zeros_like(mq)), km)
        p = jnp.exp(s - jnp.max(s, axis=1, keepdims=True))
        l = jnp.sum(p, axis=1, keepdims=True)
        mo = jnp.where(in_head, _dot(p.astype(BF16), vm) / l, mo)

    y = x_ref[...] + _dot(mix_ref[...], wo_ref[0:d, :]) + _dot(mo.astype(BF16), wo_ref[d:, :])

    yn = _rms(y, g_ref[...], NORM_EPS).astype(BF16)
    hidden = []
    for lo, hi in ff_chunks:
        gate = _dot(yn, w1_ref[:, lo:hi])
        up = _dot(yn, w1_ref[:, d_ff + lo:d_ff + hi])
        hidden.append((gate * (1.0 / (1.0 + jnp.exp(-gate))) * up).astype(BF16))
    y = y + _dot(jnp.concatenate(hidden, axis=1), w2_ref[...])
    if final_norm:
        y = _rms(y, gf_ref[...], NORM_EPS)
    o_ref[...] = y


def _ff_chunks(d_ff, n_chunks, align):
    tiles = d_ff // align
    bounds = [align * ((tiles * c) // n_chunks) for c in range(n_chunks)] + [d_ff]
    return tuple((bounds[c], bounds[c + 1]) for c in range(n_chunks))


def _post(mix2d, mq2d, x2d, km, vm, wo, g_ffn, w1, w2, g_final, *, seq, tm, final_norm):
    t, d = x2d.shape
    mw = mq2d.shape[1]
    mem_len = km.shape[1]
    d_ff = w2.shape[0]
    tiles_per_seq = seq // tm
    ff_chunks = _ff_chunks(d_ff, 2, 256)
    chunk = max(hi - lo for lo, hi in ff_chunks)
    est = ((wo.size + w1.size + w2.size) * 2 + 2 * tm * (d + mw) * 2 + 4 * tm * d * 4
           + tm * chunk * 12 + 4 * tm * d * 4)
    return pl.pallas_call(
        functools.partial(_post_kernel, ff_chunks=ff_chunks, final_norm=final_norm),
        grid=(t // tm,),
        in_specs=[
            pl.BlockSpec((tm, d), lambda i: (i, 0)),
            pl.BlockSpec((tm, mw), lambda i: (i, 0)),
            pl.BlockSpec((tm, d), lambda i: (i, 0)),
            pl.BlockSpec((1, mem_len, mw), lambda i: (i // tiles_per_seq, 0, 0)),
            pl.BlockSpec((1, mem_len, mw), lambda i: (i // tiles_per_seq, 0, 0)),
            _const_spec(wo.shape),
            _const_spec((1, d)),
            _const_spec(w1.shape),
            _const_spec(w2.shape),
            _const_spec((1, d)),
        ],
        out_specs=pl.BlockSpec((tm, d), lambda i: (i, 0)),
        out_shape=jax.ShapeDtypeStruct((t, d), F32),
        compiler_params=pltpu.CompilerParams(
            dimension_semantics=("arbitrary",), vmem_limit_bytes=_vmem_limit(est)),
        name="post",
    )(mix2d, mq2d, x2d, km, vm, wo, g_ffn.reshape(1, d), w1, w2, g_final.reshape(1, d))


def kernel(x, mem, g_mix, w_in, w_mem_kv, w_o, g_ffn, w_ffn_in, w_ffn_out,
           lam_q1, lam_k1, lam_q2, lam_k2, g_diff_head, conv_w, g_mem, g_final):
    b, seq, d = x.shape
    mem_len = mem.shape[1]
    depth = w_in.shape[0]
    t = b * seq
    tm = min(512, seq)
    sb_tq = min(1024, seq)
    diff_tq = min(1024, seq)

    km_all, vm_all = _mem_kv(mem.reshape(b * mem_len, d), g_mem, w_mem_kv.astype(BF16))
    mw = km_all.shape[-1]
    km_all = km_all.reshape(depth, b, mem_len, mw)
    vm_all = vm_all.reshape(depth, b, mem_len, mw)

    x2d = x.reshape(t, d)
    for i in range(depth):
        kind = i % N_MIXERS
        j = i // N_MIXERS
        w = w_in[i].astype(BF16)
        if kind == 2:
            mix, mq = _conv_proj(x2d, g_mix[i], w, conv_w[j], seq=seq, tm=tm)
        else:
            q, k, vt, mq = _in_proj(x2d, g_mix[i], w, q_scale=LOG2_E * HEAD_DIM ** -0.5, tm=tm)
            q, k = (a.reshape(b, seq, d) for a in (q, k))
            vt = vt.reshape(b, seq // tm, d, tm)
            if kind == 0:
                mix = _sb_attention(q, k, vt, tq=sb_tq)
            else:
                mix = _diff_attention(q, k, vt, lam_q1[j], lam_k1[j], lam_q2[j], lam_k2[j],
                                      g_diff_head[j], layer_idx=i, tq=diff_tq)
            mix = mix.reshape(t, d)
        x2d = _post(mix, mq, x2d, km_all[i], vm_all[i], w_o[i].astype(BF16), g_ffn[i],
                    w_ffn_in[i].astype(BF16), w_ffn_out[i].astype(BF16), g_final,
                    seq=seq, tm=tm, final_norm=(i == depth - 1))
    return x2d.reshape(b, seq, d)
```

```python
import functools
import math

import jax
import jax.numpy as jnp
from jax import lax
from jax.experimental import pallas as pl
from jax.experimental.pallas import tpu as pltpu

F32 = jnp.float32
BF16 = jnp.bfloat16

N_MIXERS = 3
HEAD_DIM = 128
DIFF_V_DIM = 2 * HEAD_DIM
MEM_HEADS = 4
CONV_WIDTH = 3
NORM_EPS = 1e-6
HEAD_NORM_EPS = 1e-5
MASK_VALUE = -1e30
EXP2_ARG_MAX = 126.0
F32_EXP2_UNDERFLOW_ARG = -150.0
NORM_BOUND_MARGIN = 1.0 + 2.0 ** -6
SCORE_BOUND_SLACK = 2.0
ALIBI_SLOPE_PIECES = 3
ALIBI_POS_RADIX = 16

V7X_VMEM_BYTES = 64 * 1024 * 1024
V7X_MXU_DIM = 256
V7X_MXUS = 2
LOG2_E = 1.4426950408889634
SUBLANES = 8
COL_REDUCE_WAYS = 8


def _vmem_limit(estimate_bytes):
    return int(min(max(estimate_bytes, 16 * 1024 * 1024), V7X_VMEM_BYTES - 8 * 1024 * 1024))


def _rms(xf, g, eps):
    ms = jnp.mean(xf * xf, axis=-1, keepdims=True)
    return xf * lax.rsqrt(ms + eps) * g


def _dot(a, b):
    return jnp.dot(a, b, preferred_element_type=F32)


def _dot_nt(a, b):
    return lax.dot_general(a, b, (((1,), (1,)), ((), ())), preferred_element_type=F32)


def _const_spec(shape):
    nd = len(shape)
    return pl.BlockSpec(shape, lambda *_: (0,) * nd, pipeline_mode=pl.Buffered(1))


def _mem_kv_kernel(mem_ref, g_ref, w_ref, k_ref, v_ref, *, mem_width):
    mem_n = _rms(mem_ref[...], g_ref[...], NORM_EPS).astype(BF16)
    kv = _dot(mem_n, w_ref[0])
    head_dim = mem_width // MEM_HEADS
    k_ref[0] = (kv[:, :mem_width] * head_dim ** -0.5).astype(BF16)
    v_ref[0] = kv[:, mem_width:].astype(BF16)


def _mem_kv(mem2d, g_mem, w_kv):
    depth, d_model, two_w = w_kv.shape
    mem_width = two_w // 2
    rows = mem2d.shape[0]
    return pl.pallas_call(
        functools.partial(_mem_kv_kernel, mem_width=mem_width),
        grid=(depth,),
        in_specs=[
            pl.BlockSpec((rows, d_model), lambda i: (0, 0)),
            pl.BlockSpec((1, d_model), lambda i: (0, 0)),
            pl.BlockSpec((1, d_model, two_w), lambda i: (i, 0, 0)),
        ],
        out_specs=[
            pl.BlockSpec((1, rows, mem_width), lambda i: (i, 0, 0)),
            pl.BlockSpec((1, rows, mem_width), lambda i: (i, 0, 0)),
        ],
        out_shape=[jax.ShapeDtypeStruct((depth, rows, mem_width), BF16)] * 2,
        name="mem_kv",
    )(mem2d, g_mem.reshape(1, d_model), w_kv)


def _in_proj_kernel(x_ref, g_ref, w_ref, q_ref, k_ref, vt_ref, mq_ref, *, q_scale):
    d = x_ref.shape[1]
    hn = _rms(x_ref[...], g_ref[...], NORM_EPS).astype(BF16)
    q_ref[...] = (_dot(hn, w_ref[:, 0:d]) * q_scale).astype(BF16)
    k_ref[...] = _dot(hn, w_ref[:, d:2 * d]).astype(BF16)
    vt_ref[0] = _dot(hn, w_ref[:, 2 * d:3 * d]).T.astype(BF16)
    mq_ref[...] = _dot(hn, w_ref[:, 3 * d:]).astype(BF16)


def _in_proj(x2d, g, w, *, q_scale, tm):
    t, d = x2d.shape
    n = w.shape[1]
    mw = n - 3 * d
    est = 2 * tm * d * 4 + d * n * 2 + 2 * tm * n * 2 + 3 * tm * d * 4
    return pl.pallas_call(
        functools.partial(_in_proj_kernel, q_scale=q_scale),
        grid=(t // tm,),
        in_specs=[
            pl.BlockSpec((tm, d), lambda i: (i, 0)),
            _const_spec((1, d)),
            _const_spec((d, n)),
        ],
        out_specs=[
            pl.BlockSpec((tm, d), lambda i: (i, 0)),
            pl.BlockSpec((tm, d), lambda i: (i, 0)),
            pl.BlockSpec((1, d, tm), lambda i: (i, 0, 0)),
            pl.BlockSpec((tm, mw), lambda i: (i, 0)),
        ],
        out_shape=[jax.ShapeDtypeStruct((t, d), BF16)] * 2
        + [jax.ShapeDtypeStruct((t // tm, d, tm), BF16), jax.ShapeDtypeStruct((t, mw), BF16)],
        compiler_params=pltpu.CompilerParams(
            dimension_semantics=("arbitrary",), vmem_limit_bytes=_vmem_limit(est)),
        name="in_proj",
    )(x2d, g.reshape(1, d), w)


def _conv_proj_kernel(x_ref, g_ref, w_ref, cw_ref, mix_ref, mq_ref, u_ref, *, tiles_per_seq):
    tm, d = x_ref.shape
    i = pl.program_id(0)

    @pl.when(i % tiles_per_seq == 0)
    def _():
        u_ref[0:SUBLANES, :] = jnp.zeros((SUBLANES, d), F32)

    hn = _rms(x_ref[...], g_ref[...], NORM_EPS).astype(BF16)
    gate_c = _dot(hn, w_ref[:, d:2 * d])
    h = _dot(hn, w_ref[:, 2 * d:3 * d])
    u_ref[SUBLANES:SUBLANES + tm, :] = gate_c * h
    y = cw_ref[CONV_WIDTH - 1:CONV_WIDTH, :] * u_ref[SUBLANES:SUBLANES + tm, :]
    for tap in range(CONV_WIDTH - 1):
        shift = CONV_WIDTH - 1 - tap
        y = y + cw_ref[tap:tap + 1, :] * u_ref[SUBLANES - shift:SUBLANES - shift + tm, :]
    gate_b = _dot(hn, w_ref[:, 0:d])
    mix_ref[...] = (gate_b * y).astype(BF16)
    mq_ref[...] = _dot(hn, w_ref[:, 3 * d:]).astype(BF16)
    u_ref[0:SUBLANES, :] = u_ref[tm:tm + SUBLANES, :]


def _conv_proj(x2d, g, w, conv_w, *, seq, tm):
    t, d = x2d.shape
    n = w.shape[1]
    mw = n - 3 * d
    est = 2 * tm * d * 4 + d * n * 2 + 2 * tm * (d + mw) * 2 + 5 * tm * d * 4
    return pl.pallas_call(
        functools.partial(_conv_proj_kernel, tiles_per_seq=seq // tm),
        grid=(t // tm,),
        in_specs=[
            pl.BlockSpec((tm, d), lambda i: (i, 0)),
            _const_spec((1, d)),
            _const_spec((d, n)),
            _const_spec((CONV_WIDTH, d)),
        ],
        out_specs=[
            pl.BlockSpec((tm, d), lambda i: (i, 0)),
            pl.BlockSpec((tm, mw), lambda i: (i, 0)),
        ],
        out_shape=[jax.ShapeDtypeStruct((t, d), BF16), jax.ShapeDtypeStruct((t, mw), BF16)],
        scratch_shapes=[pltpu.VMEM((tm + SUBLANES, d), F32)],
        compiler_params=pltpu.CompilerParams(
            dimension_semantics=("arbitrary",), vmem_limit_bytes=_vmem_limit(est)),
        name="conv_proj",
    )(x2d, g.reshape(1, d), w, conv_w)


def _col_reduce(x, op, reduce_fn):
    rows = x.shape[0]
    parts = [x[r * (rows // COL_REDUCE_WAYS):(r + 1) * (rows // COL_REDUCE_WAYS), :]
             for r in range(COL_REDUCE_WAYS)]
    while len(parts) > 1:
        parts = [op(parts[2 * r], parts[2 * r + 1]) for r in range(len(parts) // 2)]
    return reduce_fn(parts[0], axis=0, keepdims=True)


def _tile_validity(key_off, n_keys, qry_off, n_qry, strict):
    last_ok = key_off + n_keys - 1 < qry_off if strict else key_off + n_keys - 1 <= qry_off
    first_bad = key_off >= qry_off + n_qry - 1 if strict else key_off > qry_off + n_qry - 1
    return "all" if last_ok else ("none" if first_bad else "some")


def _sb_kernel(q_ref, k_ref, vt_ref, tri_ref, o_ref, run_ref, acc_ref, z_ref, *, tq, tk, qc, sub):
    i = pl.program_id(2)
    run_ref[...] = jnp.zeros_like(run_ref)
    acc_ref[...] = jnp.zeros_like(acc_ref)

    def load_kv(jb):
        return k_ref[0, pl.ds(pl.multiple_of(jb * tk, tk), tk), :], vt_ref[0, jb]

    def scores(c, ks):
        return jnp.minimum(_dot_nt(ks, q_ref[0, c * qc:(c + 1) * qc, :]), EXP2_ARG_MAX)

    def log_weights(c, z2, mask):
        cols = slice(c * qc, (c + 1) * qc)
        u = jnp.log2(1.0 + jnp.exp2(z2))
        if mask is not None:
            u = jnp.where(mask, u, 0.0)
        u_bf = u.astype(BF16)
        run = run_ref[:, cols]
        log_a = [None] * (tk // sub)
        for sb in reversed(range(tk // sub)):
            rows = slice(sb * sub, (sb + 1) * sub)
            suffix = _dot(tri_ref[...], u_bf[rows, :])
            log_a[sb] = z2[rows, :] - suffix - run
            run = run + suffix[0:1, :]
        run_ref[:, cols] = run
        return jnp.concatenate(log_a, axis=0)

    def accumulate(c, log_a, vt, mask):
        a = jnp.exp2(log_a)
        if mask is not None:
            a = jnp.where(mask, a, 0.0)
        acc_ref[:, c * qc:(c + 1) * qc] += _dot(vt, a.astype(BF16))

    assert (tq // tk) % 2 == 0
    n_before = i * (tq // tk)
    chunks = range(tq // qc)

    def prefetch_scores(jb, slot, active=chunks):
        ks, _ = load_kv(jnp.maximum(jb, 0))
        for c in active:
            z_ref[slot, c] = scores(c, ks)

    def process(jb, slot, active):
        _, vt = load_kv(jb)
        log_a = {c: log_weights(c, z_ref[slot, c], None) for c in active}
        prefetch_scores(jb - 1, 1 - slot, active)
        for c in active:
            accumulate(c, log_a[c], vt, None)

    key = lax.broadcasted_iota(jnp.int32, (tk, qc), 0)
    qry = lax.broadcasted_iota(jnp.int32, (tk, qc), 1)
    diag = []
    for kb in reversed(range(tq // tk)):
        for c in chunks:
            validity = _tile_validity(kb * tk, tk, c * qc, qc, strict=True)
            if validity != "none":
                diag.append((c, kb, None if validity == "all" else key + kb * tk < qry + c * qc))
    kv = {kb: load_kv(i * (tq // tk) + kb) for kb in range(tq // tk)}
    z2 = [scores(c, kv[kb][0]) for c, kb, _ in diag]
    prefetch_scores(n_before - 1, 0)
    log_a = [log_weights(c, z2[n], mask) for n, (c, _, mask) in enumerate(diag)]
    for n, (c, kb, mask) in enumerate(diag):
        accumulate(c, log_a[n], kv[kb][1], mask)

    def weights_remain(min_run):
        return min_run <= EXP2_ARG_MAX - F32_EXP2_UNDERFLOW_ARG

    def visit_key_blocks(active):
        def min_run():
            return jnp.min(run_ref[:, active[0] * qc:(active[-1] + 1) * qc])

        def keep_going(carry):
            jj, least = carry
            return jnp.logical_and(jj < n_before // 2, weights_remain(least))

        def body(carry):
            jj, _ = carry
            jb = n_before - 1 - 2 * jj
            process(jb, 0, active)

            @pl.when(weights_remain(min_run()))
            def _():
                process(jb - 1, 1, active)

            return jj + 1, min_run()

        lax.while_loop(keep_going, body, (jnp.int32(0), min_run()))

    if len(chunks) > 1:
        later_remain = weights_remain(jnp.min(run_ref[:, qc:]))

        @pl.when(later_remain)
        def _():
            visit_key_blocks(tuple(chunks))

        @pl.when(jnp.logical_not(later_remain))
        def _():
            visit_key_blocks((0,))
    else:
        visit_key_blocks(tuple(chunks))

    o_ref[0] = acc_ref[...].T.astype(o_ref.dtype)


def _sb_attention(q, k, vt, *, tq):
    b, s, d = q.shape
    tk = vt.shape[3]
    heads = d // HEAD_DIM
    qc = min(tq, V7X_MXUS * V7X_MXU_DIM)
    sub = min(tk, V7X_MXU_DIM)
    r = jnp.arange(sub)
    tri = (r[None, :] >= r[:, None]).astype(BF16)
    est = 2 * s * HEAD_DIM * 2 + 12 * (tq // qc) * tk * qc * 4 + 4 * HEAD_DIM * tq * 4
    return pl.pallas_call(
        functools.partial(_sb_kernel, tq=tq, tk=tk, qc=qc, sub=sub),
        grid=(b, heads, s // tq),
        in_specs=[
            pl.BlockSpec((1, tq, HEAD_DIM), lambda bi, h, i: (bi, i, h)),
            pl.BlockSpec((1, s, HEAD_DIM), lambda bi, h, i: (bi, 0, h), pipeline_mode=pl.Buffered(1)),
            pl.BlockSpec((1, s // tk, HEAD_DIM, tk), lambda bi, h, i: (bi, 0, h, 0),
                         pipeline_mode=pl.Buffered(1)),
            _const_spec((sub, sub)),
        ],
        out_specs=pl.BlockSpec((1, tq, HEAD_DIM), lambda bi, h, i: (bi, i, h)),
        out_shape=jax.ShapeDtypeStruct((b, s, d), BF16),
        scratch_shapes=[pltpu.VMEM((1, tq), F32), pltpu.VMEM((HEAD_DIM, tq), F32),
                        pltpu.VMEM((2, tq // qc, tk, qc), F32)],
        compiler_params=pltpu.CompilerParams(
            dimension_semantics=("arbitrary", "arbitrary", "arbitrary"),
            vmem_limit_bytes=_vmem_limit(est)),
        name="sb_attention",
    )(q, k, vt, tri)


def _diff_kernel(slope2_ref, lq1_ref, lk1_ref, lq2_ref, lk2_ref, g_ref, qf_ref, kf_ref,
                 q1_ref, q2_ref, k1_ref, k2_ref, vt_ref, o_ref, m_ref, l_ref, acc_ref, s_ref,
                 kmax2_ref,
                 *, tq, tk, qc, lambda_init):
    h = pl.program_id(1)
    i = pl.program_id(2)
    slope2 = slope2_ref[h]
    m_ref[...] = jnp.full_like(m_ref, MASK_VALUE)
    l_ref[...] = jnp.zeros_like(l_ref)
    acc_ref[...] = jnp.zeros_like(acc_ref)
    qf = jnp.broadcast_to(qf_ref[0], (qc, qf_ref.shape[2]))
    q_refs = (q1_ref, q2_ref)
    k_refs = (k1_ref, k2_ref)

    def load_kv(jb):
        start = pl.multiple_of(jb * tk, tk)
        k_aug = [jnp.concatenate([k_refs[mp][0, pl.ds(start, tk), :], kf_ref[...]], axis=1)
                 for mp in range(2)]
        return k_aug, vt_ref[0, jb]

    def scores(mp, c, k_aug):
        q_aug = jnp.concatenate([q_refs[mp][0, c * qc:(c + 1) * qc, :], qf], axis=1)
        return _dot_nt(k_aug, q_aug)

    def probabilities(mp, c, s, off, mask):
        cols = slice(c * qc, (c + 1) * qc)
        if mask is not None:
            s = jnp.where(mask, s, MASK_VALUE)
        m = m_ref[mp, :, cols]
        m_new = jnp.maximum(m, _col_reduce(s, jnp.maximum, jnp.max) + off)
        alpha = jnp.exp2(m - m_new)
        p = jnp.exp2(s - (m_new - off))
        m_ref[mp, :, cols] = m_new
        l_ref[mp, :, cols] = alpha * l_ref[mp, :, cols] + _col_reduce(p, jnp.add, jnp.sum)
        return alpha, p.astype(BF16)

    def accumulate(mp, c, alpha, p_bf, vt):
        cols = slice(c * qc, (c + 1) * qc)
        acc_ref[mp, :, cols] = alpha * acc_ref[mp, :, cols] + _dot(vt, p_bf)

    assert (tq // tk) % 2 == 0
    n_before = i * (tq // tk)
    tiles = [(mp, c) for c in range(tq // qc) for mp in range(2)]

    def prefetch_scores(jb, slot):
        k_aug, _ = load_kv(jnp.maximum(jb, 0))
        for n, t in enumerate(tiles):
            s_ref[slot, n] = scores(*t, k_aug[t[0]])

    def process(jb, slot):
        _, vt = load_kv(jb)
        off = slope2 * (jb * tk - i * tq).astype(F32)
        ap = [probabilities(*t, s_ref[slot, n], off, None) for n, t in enumerate(tiles)]
        prefetch_scores(jb - 1, 1 - slot)
        for n, t in enumerate(tiles):
            accumulate(*t, *ap[n], vt)

    key = lax.broadcasted_iota(jnp.int32, (tk, qc), 0)
    qry = lax.broadcasted_iota(jnp.int32, (tk, qc), 1)
    diag = []
    for kb in reversed(range(tq // tk)):
        for mp, c in tiles:
            validity = _tile_validity(kb * tk, tk, c * qc, qc, strict=False)
            if validity != "none":
                diag.append((mp, c, kb, None if validity == "all" else key + kb * tk <= qry + c * qc))
    kv = {kb: load_kv(i * (tq // tk) + kb) for kb in range(tq // tk)}
    s_diag = [scores(mp, c, kv[kb][0][mp]) for mp, c, kb, _ in diag]
    prefetch_scores(n_before - 1, 0)
    ap = [probabilities(mp, c, s_diag[n], slope2 * (kb * tk), mask)
          for n, (mp, c, kb, mask) in enumerate(diag)]
    for n, (mp, c, kb, _) in enumerate(diag):
        accumulate(mp, c, *ap[n], kv[kb][1])

    @pl.when(i == 0)
    def _():
        for mp in range(2):
            def norm_step(j, best):
                ks = k_refs[mp][0, pl.ds(pl.multiple_of(j * tk, tk), tk), :].astype(F32)
                best = jnp.maximum(best, jnp.max(jnp.sum(ks * ks, axis=1, keepdims=True)))
                kmax2_ref[mp, j] = best
                return best
            lax.fori_loop(0, k1_ref.shape[1] // tk, norm_step, jnp.float32(0.0))

    ones = jnp.ones((SUBLANES, q1_ref.shape[2]), BF16)
    qnorm2 = []
    for mp in range(2):
        q32 = q_refs[mp][0].astype(F32)
        qnorm2.append(_dot_nt(ones, (q32 * q32).astype(BF16))[0:1, :] * NORM_BOUND_MARGIN)

    def score_gap(jb):
        jb = jnp.maximum(jb, 0)
        gap = [jnp.max(jnp.sqrt(qnorm2[mp] * kmax2_ref[mp, jb]) - m_ref[mp]) for mp in range(2)]
        return (jnp.maximum(gap[0], gap[1]) + slope2 * (tk - 1)
                + slope2 * (jb * tk - i * tq).astype(F32))

    def keep_going(carry):
        jj, gap = carry
        return jnp.logical_and(jj < n_before // 2, gap >= F32_EXP2_UNDERFLOW_ARG - SCORE_BOUND_SLACK)

    def body(carry):
        jj, _ = carry
        jb = n_before - 1 - 2 * jj
        process(jb, 0)
        process(jb - 1, 1)
        return jj + 1, score_gap(jb - 2)

    lax.while_loop(keep_going, body, (jnp.int32(0), score_gap(n_before - 1)))

    lam = (jnp.exp(jnp.sum(lq1_ref[...] * lk1_ref[...], axis=1, keepdims=True))
           - jnp.exp(jnp.sum(lq2_ref[...] * lk2_ref[...], axis=1, keepdims=True))
           + lambda_init)
    o_t = acc_ref[0] / l_ref[0] - lam * (acc_ref[1] / l_ref[1])
    ms = jnp.mean(o_t * o_t, axis=0, keepdims=True)
    o_t = o_t * lax.rsqrt(ms + HEAD_NORM_EPS) * (g_ref[...] * (1.0 - lambda_init))
    o_ref[0] = o_t.T.astype(o_ref.dtype)


def _alibi_features(slope2, blk):
    pieces = []
    rest = slope2
    for _ in range(ALIBI_SLOPE_PIECES):
        piece = rest.astype(BF16)
        pieces.append(piece)
        rest = rest - piece.astype(F32)
    qf = jnp.stack([p * r for p in pieces for r in (ALIBI_POS_RADIX, 1)], axis=1)
    pos = jnp.arange(blk)
    digits = jnp.stack([pos // ALIBI_POS_RADIX, pos % ALIBI_POS_RADIX] * ALIBI_SLOPE_PIECES, axis=1)
    pad = HEAD_DIM - 2 * ALIBI_SLOPE_PIECES
    qf = jnp.pad(qf.astype(BF16), ((0, 0), (0, pad)))[:, None, :]
    kf = jnp.pad(digits.astype(BF16), ((0, 0), (0, pad)))
    return qf, kf


def _diff_attention(q, k, vt, lq1, lk1, lq2, lk2, g_head, *, layer_idx, tq):
    b, s, d = q.shape
    tk = vt.shape[3]
    heads = d // DIFF_V_DIM
    qc = min(tq, V7X_MXUS * V7X_MXU_DIM)
    assert tk <= ALIBI_POS_RADIX * 256
    lambda_init = 0.8 - 0.6 * math.exp(-0.3 * layer_idx)
    slope2 = LOG2_E * 2.0 ** (-8.0 * jnp.arange(1, heads + 1, dtype=F32) / heads)
    qf, kf = _alibi_features(slope2, tk)
    vec = lambda a: a.reshape(1, -1).astype(F32)
    est = (s * (2 * HEAD_DIM + DIFF_V_DIM) * 2 + 12 * 2 * (tq // qc) * tk * qc * 4
           + 4 * tq * DIFF_V_DIM * 4)
    return pl.pallas_call(
        functools.partial(_diff_kernel, tq=tq, tk=tk, qc=qc, lambda_init=lambda_init),
        grid=(b, heads, s // tq),
        in_specs=[
            pl.BlockSpec(memory_space=pltpu.SMEM),
            _const_spec((1, HEAD_DIM)), _const_spec((1, HEAD_DIM)),
            _const_spec((1, HEAD_DIM)), _const_spec((1, HEAD_DIM)),
            _const_spec((DIFF_V_DIM, 1)),
            pl.BlockSpec((1, 1, HEAD_DIM), lambda bi, h, i: (h, 0, 0)),
            _const_spec((tk, HEAD_DIM)),
            pl.BlockSpec((1, tq, HEAD_DIM), lambda bi, h, i: (bi, i, 2 * h)),
            pl.BlockSpec((1, tq, HEAD_DIM), lambda bi, h, i: (bi, i, 2 * h + 1)),
            pl.BlockSpec((1, s, HEAD_DIM), lambda bi, h, i: (bi, 0, 2 * h),
                         pipeline_mode=pl.Buffered(1)),
            pl.BlockSpec((1, s, HEAD_DIM), lambda bi, h, i: (bi, 0, 2 * h + 1),
                         pipeline_mode=pl.Buffered(1)),
            pl.BlockSpec((1, s // tk, DIFF_V_DIM, tk), lambda bi, h, i: (bi, 0, h, 0),
                         pipeline_mode=pl.Buffered(1)),
        ],
        out_specs=pl.BlockSpec((1, tq, DIFF_V_DIM), lambda bi, h, i: (bi, i, h)),
        out_shape=jax.ShapeDtypeStruct((b, s, d), BF16),
        scratch_shapes=[pltpu.VMEM((2, 1, tq), F32), pltpu.VMEM((2, 1, tq), F32),
                        pltpu.VMEM((2, DIFF_V_DIM, tq), F32),
                        pltpu.VMEM((2, 2 * (tq // qc), tk, qc), F32),
                        pltpu.SMEM((2, s // tk), F32)],
        compiler_params=pltpu.CompilerParams(
            dimension_semantics=("arbitrary", "arbitrary", "arbitrary"),
            vmem_limit_bytes=_vmem_limit(est)),
        name="diff_attention",
    )(slope2, vec(lq1), vec(lk1), vec(lq2), vec(lk2), g_head.reshape(-1, 1).astype(F32), qf, kf,
      q, q, k, k, vt)


def _post_kernel(mix_ref, mq_ref, x_ref, km_ref, vm_ref, wo_ref, g_ref, w1_ref, w2_ref, gf_ref,
                 o_ref, *, ff_chunks, final_norm):
    tm, d = x_ref.shape
    mw = mq_ref.shape[1]
    d_ff = w2_ref.shape[0]
    head_dim = mw // MEM_HEADS

    mq = mq_ref[...]
    km = km_ref[0]
    vm = vm_ref[0]
    lane_head = lax.broadcasted_iota(jnp.int32, (tm, mw), 1) // head_dim
    mo = jnp.zeros((tm, mw), F32)
    for hd in range(MEM_HEADS):
        in_head = lane_head == hd
        s = _dot_nt(jnp.where(in_head, mq, jnp.zeros_like(mq)), km)
        p = jnp.exp(s - jnp.max(s, axis=1, keepdims=True))
        l = jnp.sum(p, axis=1, keepdims=True)
        mo = jnp.where(in_head, _dot(p.astype(BF16), vm) / l, mo)

    y = x_ref[...] + _dot(mix_ref[...], wo_ref[0:d, :]) + _dot(mo.astype(BF16), wo_ref[d:, :])

    yn = _rms(y, g_ref[...], NORM_EPS).astype(BF16)
    hidden = []
    for lo, hi in ff_chunks:
        gate = _dot(yn, w1_ref[:, lo:hi])
        up = _dot(yn, w1_ref[:, d_ff + lo:d_ff + hi])
        hidden.append((gate * (1.0 / (1.0 + jnp.exp(-gate))) * up).astype(BF16))
    y = y + _dot(jnp.concatenate(hidden, axis=1), w2_ref[...])
    if final_norm:
        y = _rms(y, gf_ref[...], NORM_EPS)
    o_ref[...] = y


def _ff_chunks(d_ff, n_chunks, align):
    tiles = d_ff // align
    bounds = [align * ((tiles * c) // n_chunks) for c in range(n_chunks)] + [d_ff]
    return tuple((bounds[c], bounds[c + 1]) for c in range(n_chunks))


def _post(mix2d, mq2d, x2d, km, vm, wo, g_ffn, w1, w2, g_final, *, seq, tm, final_norm):
    t, d = x2d.shape
    mw = mq2d.shape[1]
    mem_len = km.shape[1]
    d_ff = w2.shape[0]
    tiles_per_seq = seq // tm
    ff_chunks = _ff_chunks(d_ff, 2, 256)
    chunk = max(hi - lo for lo, hi in ff_chunks)
    est = ((wo.size + w1.size + w2.size) * 2 + 2 * tm * (d + mw) * 2 + 4 * tm * d * 4
           + tm * chunk * 12 + 4 * tm * d * 4)
    return pl.pallas_call(
        functools.partial(_post_kernel, ff_chunks=ff_chunks, final_norm=final_norm),
        grid=(t // tm,),
        in_specs=[
            pl.BlockSpec((tm, d), lambda i: (i, 0)),
            pl.BlockSpec((tm, mw), lambda i: (i, 0)),
            pl.BlockSpec((tm, d), lambda i: (i, 0)),
            pl.BlockSpec((1, mem_len, mw), lambda i: (i // tiles_per_seq, 0, 0)),
            pl.BlockSpec((1, mem_len, mw), lambda i: (i // tiles_per_seq, 0, 0)),
            _const_spec(wo.shape),
            _const_spec((1, d)),
            _const_spec(w1.shape),
            _const_spec(w2.shape),
            _const_spec((1, d)),
        ],
        out_specs=pl.BlockSpec((tm, d), lambda i: (i, 0)),
        out_shape=jax.ShapeDtypeStruct((t, d), F32),
        compiler_params=pltpu.CompilerParams(
            dimension_semantics=("arbitrary",), vmem_limit_bytes=_vmem_limit(est)),
        name="post",
    )(mix2d, mq2d, x2d, km, vm, wo, g_ffn.reshape(1, d), w1, w2, g_final.reshape(1, d))


def kernel(x, mem, g_mix, w_in, w_mem_kv, w_o, g_ffn, w_ffn_in, w_ffn_out,
           lam_q1, lam_k1, lam_q2, lam_k2, g_diff_head, conv_w, g_mem, g_final):
    b, seq, d = x.shape
    mem_len = mem.shape[1]
    depth = w_in.shape[0]
    t = b * seq
    tm = min(512, seq)
    sb_tq = min(1024, seq)
    diff_tq = min(1024, seq)

    km_all, vm_all = _mem_kv(mem.reshape(b * mem_len, d), g_mem, w_mem_kv.astype(BF16))
    mw = km_all.shape[-1]
    km_all = km_all.reshape(depth, b, mem_len, mw)
    vm_all = vm_all.reshape(depth, b, mem_len, mw)

    x2d = x.reshape(t, d)
    for i in range(depth):
        kind = i % N_MIXERS
        j = i // N_MIXERS
        w = w_in[i].astype(BF16)
        if kind == 2:
            mix, mq = _conv_proj(x2d, g_mix[i], w, conv_w[j], seq=seq, tm=tm)
        else:
            q, k, vt, mq = _in_proj(x2d, g_mix[i], w, q_scale=LOG2_E * HEAD_DIM ** -0.5, tm=tm)
            q, k = (a.reshape(b, seq, d) for a in (q, k))
            vt = vt.reshape(b, seq // tm, d, tm)
            if kind == 0:
                mix = _sb_attention(q, k, vt, tq=sb_tq)
            else:
                mix = _diff_attention(q, k, vt, lam_q1[j], lam_k1[j], lam_q2[j], lam_k2[j],
                                      g_diff_head[j], layer_idx=i, tq=diff_tq)
            mix = mix.reshape(t, d)
        x2d = _post(mix, mq, x2d, km_all[i], vm_all[i], w_o[i].astype(BF16), g_ffn[i],
                    w_ffn_in[i].astype(BF16), w_ffn_out[i].astype(BF16), g_final,
                    seq=seq, tm=tm, final_norm=(i == depth - 1))
    return x2d.reshape(b, seq, d)
```

```python
import functools
import math

import jax
import jax.numpy as jnp
from jax import lax
from jax.experimental import pallas as pl
from jax.experimental.pallas import tpu as pltpu

F32 = jnp.float32
BF16 = jnp.bfloat16

N_MIXERS = 3
HEAD_DIM = 128
DIFF_V_DIM = 2 * HEAD_DIM
MEM_HEADS = 4
CONV_WIDTH = 3
NORM_EPS = 1e-6
HEAD_NORM_EPS = 1e-5
MASK_VALUE = -1e30
EXP2_ARG_MAX = 126.0
F32_EXP2_UNDERFLOW_ARG = -150.0
NORM_BOUND_MARGIN = 1.0 + 2.0 ** -6
SCORE_BOUND_SLACK = 2.0
ALIBI_SLOPE_PIECES = 3
ALIBI_POS_RADIX = 16

V7X_VMEM_BYTES = 64 * 1024 * 1024
V7X_MXU_DIM = 256
V7X_MXUS = 2
LOG2_E = 1.4426950408889634
SUBLANES = 8
COL_REDUCE_WAYS = 8


def _vmem_limit(estimate_bytes):
    return int(min(max(estimate_bytes, 16 * 1024 * 1024), V7X_VMEM_BYTES - 8 * 1024 * 1024))


def _rms(xf, g, eps):
    ms = jnp.mean(xf * xf, axis=-1, keepdims=True)
    return xf * lax.rsqrt(ms + eps) * g


def _dot(a, b):
    return jnp.dot(a, b, preferred_element_type=F32)


def _dot_nt(a, b):
    return lax.dot_general(a, b, (((1,), (1,)), ((), ())), preferred_element_type=F32)


def _const_spec(shape):
    nd = len(shape)
    return pl.BlockSpec(shape, lambda *_: (0,) * nd, pipeline_mode=pl.Buffered(1))


def _mem_kv_kernel(mem_ref, g_ref, w_ref, k_ref, v_ref, *, mem_width):
    mem_n = _rms(mem_ref[...], g_ref[...], NORM_EPS).astype(BF16)
    kv = _dot(mem_n, w_ref[0])
    head_dim = mem_width // MEM_HEADS
    k_ref[0] = (kv[:, :mem_width] * head_dim ** -0.5).astype(BF16)
    v_ref[0] = kv[:, mem_width:].astype(BF16)


def _mem_kv(mem2d, g_mem, w_kv):
    depth, d_model, two_w = w_kv.shape
    mem_width = two_w // 2
    rows = mem2d.shape[0]
    return pl.pallas_call(
        functools.partial(_mem_kv_kernel, mem_width=mem_width),
        grid=(depth,),
        in_specs=[
            pl.BlockSpec((rows, d_model), lambda i: (0, 0)),
            pl.BlockSpec((1, d_model), lambda i: (0, 0)),
            pl.BlockSpec((1, d_model, two_w), lambda i: (i, 0, 0)),
        ],
        out_specs=[
            pl.BlockSpec((1, rows, mem_width), lambda i: (i, 0, 0)),
            pl.BlockSpec((1, rows, mem_width), lambda i: (i, 0, 0)),
        ],
        out_shape=[jax.ShapeDtypeStruct((depth, rows, mem_width), BF16)] * 2,
        name="mem_kv",
    )(mem2d, g_mem.reshape(1, d_model), w_kv)


def _in_proj_kernel(x_ref, g_ref, w_ref, q_ref, k_ref, vt_ref, mq_ref, *, q_scale):
    d = x_ref.shape[1]
    hn = _rms(x_ref[...], g_ref[...], NORM_EPS).astype(BF16)
    q_ref[...] = (_dot(hn, w_ref[:, 0:d]) * q_scale).astype(BF16)
    k_ref[...] = _dot(hn, w_ref[:, d:2 * d]).astype(BF16)
    vt_ref[0] = _dot(hn, w_ref[:, 2 * d:3 * d]).T.astype(BF16)
    mq_ref[...] = _dot(hn, w_ref[:, 3 * d:]).astype(BF16)


def _in_proj(x2d, g, w, *, q_scale, tm):
    t, d = x2d.shape
    n = w.shape[1]
    mw = n - 3 * d
    est = 2 * tm * d * 4 + d * n * 2 + 2 * tm * n * 2 + 3 * tm * d * 4
    return pl.pallas_call(
        functools.partial(_in_proj_kernel, q_scale=q_scale),
        grid=(t // tm,),
        in_specs=[
            pl.BlockSpec((tm, d), lambda i: (i, 0)),
            _const_spec((1, d)),
            _const_spec((d, n)),
        ],
        out_specs=[
            pl.BlockSpec((tm, d), lambda i: (i, 0)),
            pl.BlockSpec((tm, d), lambda i: (i, 0)),
            pl.BlockSpec((1, d, tm), lambda i: (i, 0, 0)),
            pl.BlockSpec((tm, mw), lambda i: (i, 0)),
        ],
        out_shape=[jax.ShapeDtypeStruct((t, d), BF16)] * 2
        + [jax.ShapeDtypeStruct((t // tm, d, tm), BF16), jax.ShapeDtypeStruct((t, mw), BF16)],
        compiler_params=pltpu.CompilerParams(
            dimension_semantics=("arbitrary",), vmem_limit_bytes=_vmem_limit(est)),
        name="in_proj",
    )(x2d, g.reshape(1, d), w)


def _conv_proj_kernel(x_ref, g_ref, w_ref, cw_ref, mix_ref, mq_ref, u_ref, *, tiles_per_seq):
    tm, d = x_ref.shape
    i = pl.program_id(0)

    @pl.when(i % tiles_per_seq == 0)
    def _():
        u_ref[0:SUBLANES, :] = jnp.zeros((SUBLANES, d), F32)

    hn = _rms(x_ref[...], g_ref[...], NORM_EPS).astype(BF16)
    gate_c = _dot(hn, w_ref[:, d:2 * d])
    h = _dot(hn, w_ref[:, 2 * d:3 * d])
    u_ref[SUBLANES:SUBLANES + tm, :] = gate_c * h
    y = cw_ref[CONV_WIDTH - 1:CONV_WIDTH, :] * u_ref[SUBLANES:SUBLANES + tm, :]
    for tap in range(CONV_WIDTH - 1):
        shift = CONV_WIDTH - 1 - tap
        y = y + cw_ref[tap:tap + 1, :] * u_ref[SUBLANES - shift:SUBLANES - shift + tm, :]
    gate_b = _dot(hn, w_ref[:, 0:d])
    mix_ref[...] = (gate_b * y).astype(BF16)
    mq_ref[...] = _dot(hn, w_ref[:, 3 * d:]).astype(BF16)
    u_ref[0:SUBLANES, :] = u_ref[tm:tm + SUBLANES, :]


def _conv_proj(x2d, g, w, conv_w, *, seq, tm):
    t, d = x2d.shape
    n = w.shape[1]
    mw = n - 3 * d
    est = 2 * tm * d * 4 + d * n * 2 + 2 * tm * (d + mw) * 2 + 5 * tm * d * 4
    return pl.pallas_call(
        functools.partial(_conv_proj_kernel, tiles_per_seq=seq // tm),
        grid=(t // tm,),
        in_specs=[
            pl.BlockSpec((tm, d), lambda i: (i, 0)),
            _const_spec((1, d)),
            _const_spec((d, n)),
            _const_spec((CONV_WIDTH, d)),
        ],
        out_specs=[
            pl.BlockSpec((tm, d), lambda i: (i, 0)),
            pl.BlockSpec((tm, mw), lambda i: (i, 0)),
        ],
        out_shape=[jax.ShapeDtypeStruct((t, d), BF16), jax.ShapeDtypeStruct((t, mw), BF16)],
        scratch_shapes=[pltpu.VMEM((tm + SUBLANES, d), F32)],
        compiler_params=pltpu.CompilerParams(
            dimension_semantics=("arbitrary",), vmem_limit_bytes=_vmem_limit(est)),
        name="conv_proj",
    )(x2d, g.reshape(1, d), w, conv_w)


def _col_reduce(x, op, reduce_fn):
    rows = x.shape[0]
    parts = [x[r * (rows // COL_REDUCE_WAYS):(r + 1) * (rows // COL_REDUCE_WAYS), :]
             for r in range(COL_REDUCE_WAYS)]
    while len(parts) > 1:
        parts = [op(parts[2 * r], parts[2 * r + 1]) for r in range(len(parts) // 2)]
    return reduce_fn(parts[0], axis=0, keepdims=True)


def _tile_validity(key_off, n_keys, qry_off, n_qry, strict):
    last_ok = key_off + n_keys - 1 < qry_off if strict else key_off + n_keys - 1 <= qry_off
    first_bad = key_off >= qry_off + n_qry - 1 if strict else key_off > qry_off + n_qry - 1
    return "all" if last_ok else ("none" if first_bad else "some")


def _sb_kernel(q_ref, k_ref, vt_ref, tri_ref, o_ref, run_ref, acc_ref, z_ref, *, tq, tk, qc, sub):
    i = pl.program_id(2)
    run_ref[...] = jnp.zeros_like(run_ref)
    acc_ref[...] = jnp.zeros_like(acc_ref)

    def load_kv(jb):
        return k_ref[0, pl.ds(pl.multiple_of(jb * tk, tk), tk), :], vt_ref[0, jb]

    def scores(c, ks):
        return jnp.minimum(_dot_nt(ks, q_ref[0, c * qc:(c + 1) * qc, :]), EXP2_ARG_MAX)

    def log_weights(c, z2, mask):
        cols = slice(c * qc, (c + 1) * qc)
        u = jnp.log2(1.0 + jnp.exp2(z2))
        if mask is not None:
            u = jnp.where(mask, u, 0.0)
        u_bf = u.astype(BF16)
        run = run_ref[:, cols]
        log_a = [None] * (tk // sub)
        for sb in reversed(range(tk // sub)):
            rows = slice(sb * sub, (sb + 1) * sub)
            suffix = _dot(tri_ref[...], u_bf[rows, :])
            log_a[sb] = z2[rows, :] - suffix - run
            run = run + suffix[0:1, :]
        run_ref[:, cols] = run
        return jnp.concatenate(log_a, axis=0)

    def accumulate(c, log_a, vt, mask):
        a = jnp.exp2(log_a)
        if mask is not None:
            a = jnp.where(mask, a, 0.0)
        acc_ref[:, c * qc:(c + 1) * qc] += _dot(vt, a.astype(BF16))

    assert (tq // tk) % 2 == 0
    n_before = i * (tq // tk)
    chunks = range(tq // qc)

    def prefetch_scores(jb, slot, active=chunks):
        ks, _ = load_kv(jnp.maximum(jb, 0))
        for c in active:
            z_ref[slot, c] = scores(c, ks)

    def process(jb, slot, active):
        _, vt = load_kv(jb)
        log_a = {c: log_weights(c, z_ref[slot, c], None) for c in active}
        prefetch_scores(jb - 1, 1 - slot, active)
        for c in active:
            accumulate(c, log_a[c], vt, None)

    key = lax.broadcasted_iota(jnp.int32, (tk, qc), 0)
    qry = lax.broadcasted_iota(jnp.int32, (tk, qc), 1)
    diag = []
    for kb in reversed(range(tq // tk)):
        for c in chunks:
            validity = _tile_validity(kb * tk, tk, c * qc, qc, strict=True)
            if validity != "none":
                diag.append((c, kb, None if validity == "all" else key + kb * tk < qry + c * qc))
    kv = {kb: load_kv(i * (tq // tk) + kb) for kb in range(tq // tk)}
    z2 = [scores(c, kv[kb][0]) for c, kb, _ in diag]
    prefetch_scores(n_before - 1, 0)
    log_a = [log_weights(c, z2[n], mask) for n, (c, _, mask) in enumerate(diag)]
    for n, (c, kb, mask) in enumerate(diag):
        accumulate(c, log_a[n], kv[kb][1], mask)

    def weights_remain(min_run):
        return min_run <= EXP2_ARG_MAX - F32_EXP2_UNDERFLOW_ARG

    def visit_key_blocks(active):
        def min_run():
            return jnp.min(run_ref[:, active[0] * qc:(active[-1] + 1) * qc])

        def keep_going(carry):
            jj, least = carry
            return jnp.logical_and(jj < n_before // 2, weights_remain(least))

        def body(carry):
            jj, _ = carry
            jb = n_before - 1 - 2 * jj
            process(jb, 0, active)

            @pl.when(weights_remain(min_run()))
            def _():
                process(jb - 1, 1, active)

            return jj + 1, min_run()

        lax.while_loop(keep_going, body, (jnp.int32(0), min_run()))

    if len(chunks) > 1:
        later_remain = weights_remain(jnp.min(run_ref[:, qc:]))

        @pl.when(later_remain)
        def _():
            visit_key_blocks(tuple(chunks))

        @pl.when(jnp.logical_not(later_remain))
        def _():
            visit_key_blocks((0,))
    else:
        visit_key_blocks(tuple(chunks))

    o_ref[0] = acc_ref[...].T.astype(o_ref.dtype)


def _sb_attention(q, k, vt, *, tq):
    b, s, d = q.shape
    tk = vt.shape[3]
    heads = d // HEAD_DIM
    qc = min(tq, V7X_MXUS * V7X_MXU_DIM)
    sub = min(tk, V7X_MXU_DIM)
    r = jnp.arange(sub)
    tri = (r[None, :] >= r[:, None]).astype(BF16)
    est = 2 * s * HEAD_DIM * 2 + 12 * (tq // qc) * tk * qc * 4 + 4 * HEAD_DIM * tq * 4
    return pl.pallas_call(
        functools.partial(_sb_kernel, tq=tq, tk=tk, qc=qc, sub=sub),
        grid=(b, heads, s // tq),
        in_specs=[
            pl.BlockSpec((1, tq, HEAD_DIM), lambda bi, h, i: (bi, i, h)),
            pl.BlockSpec((1, s, HEAD_DIM), lambda bi, h, i: (bi, 0, h), pipeline_mode=pl.Buffered(1)),
            pl.BlockSpec((1, s // tk, HEAD_DIM, tk), lambda bi, h, i: (bi, 0, h, 0),
                         pipeline_mode=pl.Buffered(1)),
            _const_spec((sub, sub)),
        ],
        out_specs=pl.BlockSpec((1, tq, HEAD_DIM), lambda bi, h, i: (bi, i, h)),
        out_shape=jax.ShapeDtypeStruct((b, s, d), BF16),
        scratch_shapes=[pltpu.VMEM((1, tq), F32), pltpu.VMEM((HEAD_DIM, tq), F32),
                        pltpu.VMEM((2, tq // qc, tk, qc), F32)],
        compiler_params=pltpu.CompilerParams(
            dimension_semantics=("arbitrary", "arbitrary", "arbitrary"),
            vmem_limit_bytes=_vmem_limit(est)),
        name="sb_attention",
    )(q, k, vt, tri)


def _diff_kernel(slope2_ref, lq1_ref, lk1_ref, lq2_ref, lk2_ref, g_ref, qf_ref, kf_ref,
                 q1_ref, q2_ref, k1_ref, k2_ref, vt_ref, o_ref, m_ref, l_ref, acc_ref, s_ref,
                 kmax2_ref,
                 *, tq, tk, qc, lambda_init):
    h = pl.program_id(1)
    i = pl.program_id(2)
    slope2 = slope2_ref[h]
    m_ref[...] = jnp.full_like(m_ref, MASK_VALUE)
    l_ref[...] = jnp.zeros_like(l_ref)
    acc_ref[...] = jnp.zeros_like(acc_ref)
    qf = jnp.broadcast_to(qf_ref[0], (qc, qf_ref.shape[2]))
    q_refs = (q1_ref, q2_ref)
    k_refs = (k1_ref, k2_ref)

    def load_kv(jb):
        start = pl.multiple_of(jb * tk, tk)
        k_aug = [jnp.concatenate([k_refs[mp][0, pl.ds(start, tk), :], kf_ref[...]], axis=1)
                 for mp in range(2)]
        return k_aug, vt_ref[0, jb]

    def scores(mp, c, k_aug):
        q_aug = jnp.concatenate([q_refs[mp][0, c * qc:(c + 1) * qc, :], qf], axis=1)
        return _dot_nt(k_aug, q_aug)

    def probabilities(mp, c, s, off, mask):
        cols = slice(c * qc, (c + 1) * qc)
        if mask is not None:
            s = jnp.where(mask, s, MASK_VALUE)
        m = m_ref[mp, :, cols]
        m_new = jnp.maximum(m, _col_reduce(s, jnp.maximum, jnp.max) + off)
        alpha = jnp.exp2(m - m_new)
        p = jnp.exp2(s - (m_new - off))
        m_ref[mp, :, cols] = m_new
        l_ref[mp, :, cols] = alpha * l_ref[mp, :, cols] + _col_reduce(p, jnp.add, jnp.sum)
        return alpha, p.astype(BF16)

    def accumulate(mp, c, alpha, p_bf, vt):
        cols = slice(c * qc, (c + 1) * qc)
        acc_ref[mp, :, cols] = alpha * acc_ref[mp, :, cols] + _dot(vt, p_bf)

    assert (tq // tk) % 2 == 0
    n_before = i * (tq // tk)
    tiles = [(mp, c) for c in range(tq // qc) for mp in range(2)]

    def prefetch_scores(jb, slot):
        k_aug, _ = load_kv(jnp.maximum(jb, 0))
        for n, t in enumerate(tiles):
            s_ref[slot, n] = scores(*t, k_aug[t[0]])

    def process(jb, slot):
        _, vt = load_kv(jb)
        off = slope2 * (jb * tk - i * tq).astype(F32)
        k_next, _ = load_kv(jnp.maximum(jb - 1, 0))
        for n, t in enumerate(tiles):
            accumulate(*t, *probabilities(*t, s_ref[slot, n], off, None), vt)
            s_ref[1 - slot, n] = scores(*t, k_next[t[0]])

    key = lax.broadcasted_iota(jnp.int32, (tk, qc), 0)
    qry = lax.broadcasted_iota(jnp.int32, (tk, qc), 1)
    diag = []
    for kb in reversed(range(tq // tk)):
        for mp, c in tiles:
            validity = _tile_validity(kb * tk, tk, c * qc, qc, strict=False)
            if validity != "none":
                diag.append((mp, c, kb, None if validity == "all" else key + kb * tk <= qry + c * qc))
    kv = {kb: load_kv(i * (tq // tk) + kb) for kb in range(tq // tk)}
    s_diag = [scores(mp, c, kv[kb][0][mp]) for mp, c, kb, _ in diag]
    prefetch_scores(n_before - 1, 0)
    ap = [probabilities(mp, c, s_diag[n], slope2 * (kb * tk), mask)
          for n, (mp, c, kb, mask) in enumerate(diag)]
    for n, (mp, c, kb, _) in enumerate(diag):
        accumulate(mp, c, *ap[n], kv[kb][1])

    @pl.when(i == 0)
    def _():
        for mp in range(2):
            def norm_step(j, best):
                ks = k_refs[mp][0, pl.ds(pl.multiple_of(j * tk, tk), tk), :].astype(F32)
                best = jnp.maximum(best, jnp.max(jnp.sum(ks * ks, axis=1, keepdims=True)))
                kmax2_ref[mp, j] = best
                return best
            lax.fori_loop(0, k1_ref.shape[1] // tk, norm_step, jnp.float32(0.0))

    ones = jnp.ones((SUBLANES, q1_ref.shape[2]), BF16)
    qnorm2 = []
    for mp in range(2):
        q32 = q_refs[mp][0].astype(F32)
        qnorm2.append(_dot_nt(ones, (q32 * q32).astype(BF16))[0:1, :] * NORM_BOUND_MARGIN)

    def score_gap(jb):
        jb = jnp.maximum(jb, 0)
        gap = [jnp.max(jnp.sqrt(qnorm2[mp] * kmax2_ref[mp, jb]) - m_ref[mp]) for mp in range(2)]
        return (jnp.maximum(gap[0], gap[1]) + slope2 * (tk - 1)
                + slope2 * (jb * tk - i * tq).astype(F32))

    def keep_going(carry):
        jj, gap = carry
        return jnp.logical_and(jj < n_before // 2, gap >= F32_EXP2_UNDERFLOW_ARG - SCORE_BOUND_SLACK)

    def body(carry):
        jj, _ = carry
        jb = n_before - 1 - 2 * jj
        process(jb, 0)
        process(jb - 1, 1)
        return jj + 1, score_gap(jb - 2)

    lax.while_loop(keep_going, body, (jnp.int32(0), score_gap(n_before - 1)))

    lam = (jnp.exp(jnp.sum(lq1_ref[...] * lk1_ref[...], axis=1, keepdims=True))
           - jnp.exp(jnp.sum(lq2_ref[...] * lk2_ref[...], axis=1, keepdims=True))
           + lambda_init)
    o_t = acc_ref[0] / l_ref[0] - lam * (acc_ref[1] / l_ref[1])
    ms = jnp.mean(o_t * o_t, axis=0, keepdims=True)
    o_t = o_t * lax.rsqrt(ms + HEAD_NORM_EPS) * (g_ref[...] * (1.0 - lambda_init))
    o_ref[0] = o_t.T.astype(o_ref.dtype)


def _alibi_features(slope2, blk):
    pieces = []
    rest = slope2
    for _ in range(ALIBI_SLOPE_PIECES):
        piece = rest.astype(BF16)
        pieces.append(piece)
        rest = rest - piece.astype(F32)
    qf = jnp.stack([p * r for p in pieces for r in (ALIBI_POS_RADIX, 1)], axis=1)
    pos = jnp.arange(blk)
    digits = jnp.stack([pos // ALIBI_POS_RADIX, pos % ALIBI_POS_RADIX] * ALIBI_SLOPE_PIECES, axis=1)
    pad = HEAD_DIM - 2 * ALIBI_SLOPE_PIECES
    qf = jnp.pad(qf.astype(BF16), ((0, 0), (0, pad)))[:, None, :]
    kf = jnp.pad(digits.astype(BF16), ((0, 0), (0, pad)))
    return qf, kf


def _diff_attention(q, k, vt, lq1, lk1, lq2, lk2, g_head, *, layer_idx, tq):
    b, s, d = q.shape
    tk = vt.shape[3]
    heads = d // DIFF_V_DIM
    qc = min(tq, V7X_MXUS * V7X_MXU_DIM)
    assert tk <= ALIBI_POS_RADIX * 256
    lambda_init = 0.8 - 0.6 * math.exp(-0.3 * layer_idx)
    slope2 = LOG2_E * 2.0 ** (-8.0 * jnp.arange(1, heads + 1, dtype=F32) / heads)
    qf, kf = _alibi_features(slope2, tk)
    vec = lambda a: a.reshape(1, -1).astype(F32)
    est = (s * (2 * HEAD_DIM + DIFF_V_DIM) * 2 + 12 * 2 * (tq // qc) * tk * qc * 4
           + 4 * tq * DIFF_V_DIM * 4)
    return pl.pallas_call(
        functools.partial(_diff_kernel, tq=tq, tk=tk, qc=qc, lambda_init=lambda_init),
        grid=(b, heads, s // tq),
        in_specs=[
            pl.BlockSpec(memory_space=pltpu.SMEM),
            _const_spec((1, HEAD_DIM)), _const_spec((1, HEAD_DIM)),
            _const_spec((1, HEAD_DIM)), _const_spec((1, HEAD_DIM)),
            _const_spec((DIFF_V_DIM, 1)),
            pl.BlockSpec((1, 1, HEAD_DIM), lambda bi, h, i: (h, 0, 0)),
            _const_spec((tk, HEAD_DIM)),
            pl.BlockSpec((1, tq, HEAD_DIM), lambda bi, h, i: (bi, i, 2 * h)),
            pl.BlockSpec((1, tq, HEAD_DIM), lambda bi, h, i: (bi, i, 2 * h + 1)),
            pl.BlockSpec((1, s, HEAD_DIM), lambda bi, h, i: (bi, 0, 2 * h),
                         pipeline_mode=pl.Buffered(1)),
            pl.BlockSpec((1, s, HEAD_DIM), lambda bi, h, i: (bi, 0, 2 * h + 1),
                         pipeline_mode=pl.Buffered(1)),
            pl.BlockSpec((1, s // tk, DIFF_V_DIM, tk), lambda bi, h, i: (bi, 0, h, 0),
                         pipeline_mode=pl.Buffered(1)),
        ],
        out_specs=pl.BlockSpec((1, tq, DIFF_V_DIM), lambda bi, h, i: (bi, i, h)),
        out_shape=jax.ShapeDtypeStruct((b, s, d), BF16),
        scratch_shapes=[pltpu.VMEM((2, 1, tq), F32), pltpu.VMEM((2, 1, tq), F32),
                        pltpu.VMEM((2, DIFF_V_DIM, tq), F32),
                        pltpu.VMEM((2, 2 * (tq // qc), tk, qc), F32),
                        pltpu.SMEM((2, s // tk), F32)],
        compiler_params=pltpu.CompilerParams(
            dimension_semantics=("arbitrary", "arbitrary", "arbitrary"),
            vmem_limit_bytes=_vmem_limit(est)),
        name="diff_attention",
    )(slope2, vec(lq1), vec(lk1), vec(lq2), vec(lk2), g_head.reshape(-1, 1).astype(F32), qf, kf,
      q, q, k, k, vt)


def _post_kernel(mix_ref, mq_ref, x_ref, km_ref, vm_ref, wo_ref, g_ref, w1_ref, w2_ref, gf_ref,
                 o_ref, *, ff_chunks, final_norm):
    tm, d = x_ref.shape
    mw = mq_ref.shape[1]
    d_ff = w2_ref.shape[0]
    head_dim = mw // MEM_HEADS

    mq = mq_ref[...]
    km = km_ref[0]
    vm = vm_ref[0]
    lane_head = lax.broadcasted_iota(jnp.int32, (tm, mw), 1) // head_dim
    mo = jnp.zeros((tm, mw), F32)
    for hd in range(MEM_HEADS):
        in_head = lane_head == hd
        s = _dot_nt(jnp.where(in_head, mq, jnp.zeros_like(mq)), km)
        p = jnp.exp(s - jnp.max(s, axis=1, keepdims=True))
        l = jnp.sum(p, axis=1, keepdims=True)
        mo = jnp.where(in_head, _dot(p.astype(BF16), vm) / l, mo)

    y = x_ref[...] + _dot(mix_ref[...], wo_ref[0:d, :]) + _dot(mo.astype(BF16), wo_ref[d:, :])

    yn = _rms(y, g_ref[...], NORM_EPS).astype(BF16)
    hidden = []
    for lo, hi in ff_chunks:
        gate = _dot(yn, w1_ref[:, lo:hi])
        up = _dot(yn, w1_ref[:, d_ff + lo:d_ff + hi])
        hidden.append((gate * (1.0 / (1.0 + jnp.exp(-gate))) * up).astype(BF16))
    y = y + _dot(jnp.concatenate(hidden, axis=1), w2_ref[...])
    if final_norm:
        y = _rms(y, gf_ref[...], NORM_EPS)
    o_ref[...] = y


def _ff_chunks(d_ff, n_chunks, align):
    tiles = d_ff // align
    bounds = [align * ((tiles * c) // n_chunks) for c in range(n_chunks)] + [d_ff]
    return tuple((bounds[c], bounds[c + 1]) for c in range(n_chunks))


def _post(mix2d, mq2d, x2d, km, vm, wo, g_ffn, w1, w2, g_final, *, seq, tm, final_norm):
    t, d = x2d.shape
    mw = mq2d.shape[1]
    mem_len = km.shape[1]
    d_ff = w2.shape[0]
    tiles_per_seq = seq // tm
    ff_chunks = _ff_chunks(d_ff, 2, 256)
    chunk = max(hi - lo for lo, hi in ff_chunks)
    est = ((wo.size + w1.size + w2.size) * 2 + 2 * tm * (d + mw) * 2 + 4 * tm * d * 4
           + tm * chunk * 12 + 4 * tm * d * 4)
    return pl.pallas_call(
        functools.partial(_post_kernel, ff_chunks=ff_chunks, final_norm=final_norm),
        grid=(t // tm,),
        in_specs=[
            pl.BlockSpec((tm, d), lambda i: (i, 0)),
            pl.BlockSpec((tm, mw), lambda i: (i, 0)),
            pl.BlockSpec((tm, d), lambda i: (i, 0)),
            pl.BlockSpec((1, mem_len, mw), lambda i: (i // tiles_per_seq, 0, 0)),
            pl.BlockSpec((1, mem_len, mw), lambda i: (i // tiles_per_seq, 0, 0)),
            _const_spec(wo.shape),
            _const_spec((1, d)),
            _const_spec(w1.shape),
            _const_spec(w2.shape),
            _const_spec((1, d)),
        ],
        out_specs=pl.BlockSpec((tm, d), lambda i: (i, 0)),
        out_shape=jax.ShapeDtypeStruct((t, d), F32),
        compiler_params=pltpu.CompilerParams(
            dimension_semantics=("arbitrary",), vmem_limit_bytes=_vmem_limit(est)),
        name="post",
    )(mix2d, mq2d, x2d, km, vm, wo, g_ffn.reshape(1, d), w1, w2, g_final.reshape(1, d))


def kernel(x, mem, g_mix, w_in, w_mem_kv, w_o, g_ffn, w_ffn_in, w_ffn_out,
           lam_q1, lam_k1, lam_q2, lam_k2, g_diff_head, conv_w, g_mem, g_final):
    b, seq, d = x.shape
    mem_len = mem.shape[1]
    depth = w_in.shape[0]
    t = b * seq
    tm = min(512, seq)
    sb_tq = min(1024, seq)
    diff_tq = min(1024, seq)

    km_all, vm_all = _mem_kv(mem.reshape(b * mem_len, d), g_mem, w_mem_kv.astype(BF16))
    mw = km_all.shape[-1]
    km_all = km_all.reshape(depth, b, mem_len, mw)
    vm_all = vm_all.reshape(depth, b, mem_len, mw)

    x2d = x.reshape(t, d)
    for i in range(depth):
        kind = i % N_MIXERS
        j = i // N_MIXERS
        w = w_in[i].astype(BF16)
        if kind == 2:
            mix, mq = _conv_proj(x2d, g_mix[i], w, conv_w[j], seq=seq, tm=tm)
        else:
            q, k, vt, mq = _in_proj(x2d, g_mix[i], w, q_scale=LOG2_E * HEAD_DIM ** -0.5, tm=tm)
            q, k = (a.reshape(b, seq, d) for a in (q, k))
            vt = vt.reshape(b, seq // tm, d, tm)
            if kind == 0:
                mix = _sb_attention(q, k, vt, tq=sb_tq)
            else:
                mix = _diff_attention(q, k, vt, lam_q1[j], lam_k1[j], lam_q2[j], lam_k2[j],
                                      g_diff_head[j], layer_idx=i, tq=diff_tq)
            mix = mix.reshape(t, d)
        x2d = _post(mix, mq, x2d, km_all[i], vm_all[i], w_o[i].astype(BF16), g_ffn[i],
                    w_ffn_in[i].astype(BF16), w_ffn_out[i].astype(BF16), g_final,
                    seq=seq, tm=tm, final_norm=(i == depth - 1))
    return x2d.reshape(b, seq, d)
```

```python
import functools
import math

import jax
import jax.numpy as jnp
from jax import lax
from jax.experimental import pallas as pl
from jax.experimental.pallas import tpu as pltpu

F32 = jnp.float32
BF16 = jnp.bfloat16

N_MIXERS = 3
HEAD_DIM = 128
DIFF_V_DIM = 2 * HEAD_DIM
MEM_HEADS = 4
CONV_WIDTH = 3
NORM_EPS = 1e-6
HEAD_NORM_EPS = 1e-5
MASK_VALUE = -1e30
EXP2_ARG_MAX = 126.0
F32_EXP2_UNDERFLOW_ARG = -150.0
NORM_BOUND_MARGIN = 1.0 + 2.0 ** -6
SCORE_BOUND_SLACK = 2.0
ALIBI_SLOPE_PIECES = 3
ALIBI_POS_RADIX = 16

V7X_VMEM_BYTES = 64 * 1024 * 1024
V7X_MXU_DIM = 256
V7X_MXUS = 2
LOG2_E = 1.4426950408889634
SUBLANES = 8
COL_REDUCE_WAYS = 8


def _vmem_limit(estimate_bytes):
    return int(min(max(estimate_bytes, 16 * 1024 * 1024), V7X_VMEM_BYTES - 8 * 1024 * 1024))


def _rms(xf, g, eps):
    ms = jnp.mean(xf * xf, axis=-1, keepdims=True)
    return xf * lax.rsqrt(ms + eps) * g


def _dot(a, b):
    return jnp.dot(a, b, preferred_element_type=F32)


def _dot_nt(a, b):
    return lax.dot_general(a, b, (((1,), (1,)), ((), ())), preferred_element_type=F32)


def _const_spec(shape):
    nd = len(shape)
    return pl.BlockSpec(shape, lambda *_: (0,) * nd, pipeline_mode=pl.Buffered(1))


def _mem_kv_kernel(mem_ref, g_ref, w_ref, k_ref, v_ref, *, mem_width):
    mem_n = _rms(mem_ref[...], g_ref[...], NORM_EPS).astype(BF16)
    kv = _dot(mem_n, w_ref[0])
    head_dim = mem_width // MEM_HEADS
    k_ref[0] = (kv[:, :mem_width] * head_dim ** -0.5).astype(BF16)
    v_ref[0] = kv[:, mem_width:].astype(BF16)


def _mem_kv(mem2d, g_mem, w_kv):
    depth, d_model, two_w = w_kv.shape
    mem_width = two_w // 2
    rows = mem2d.shape[0]
    return pl.pallas_call(
        functools.partial(_mem_kv_kernel, mem_width=mem_width),
        grid=(depth,),
        in_specs=[
            pl.BlockSpec((rows, d_model), lambda i: (0, 0)),
            pl.BlockSpec((1, d_model), lambda i: (0, 0)),
            pl.BlockSpec((1, d_model, two_w), lambda i: (i, 0, 0)),
        ],
        out_specs=[
            pl.BlockSpec((1, rows, mem_width), lambda i: (i, 0, 0)),
            pl.BlockSpec((1, rows, mem_width), lambda i: (i, 0, 0)),
        ],
        out_shape=[jax.ShapeDtypeStruct((depth, rows, mem_width), BF16)] * 2,
        name="mem_kv",
    )(mem2d, g_mem.reshape(1, d_model), w_kv)


def _in_proj_kernel(x_ref, g_ref, w_ref, q_ref, k_ref, vt_ref, mq_ref, *, q_scale):
    d = x_ref.shape[1]
    hn = _rms(x_ref[...], g_ref[...], NORM_EPS).astype(BF16)
    q_ref[...] = (_dot(hn, w_ref[:, 0:d]) * q_scale).astype(BF16)
    k_ref[...] = _dot(hn, w_ref[:, d:2 * d]).astype(BF16)
    vt_ref[0] = _dot(hn, w_ref[:, 2 * d:3 * d]).T.astype(BF16)
    mq_ref[...] = _dot(hn, w_ref[:, 3 * d:]).astype(BF16)


def _in_proj(x2d, g, w, *, q_scale, tm):
    t, d = x2d.shape
    n = w.shape[1]
    mw = n - 3 * d
    est = 2 * tm * d * 4 + d * n * 2 + 2 * tm * n * 2 + 3 * tm * d * 4
    return pl.pallas_call(
        functools.partial(_in_proj_kernel, q_scale=q_scale),
        grid=(t // tm,),
        in_specs=[
            pl.BlockSpec((tm, d), lambda i: (i, 0)),
            _const_spec((1, d)),
            _const_spec((d, n)),
        ],
        out_specs=[
            pl.BlockSpec((tm, d), lambda i: (i, 0)),
            pl.BlockSpec((tm, d), lambda i: (i, 0)),
            pl.BlockSpec((1, d, tm), lambda i: (i, 0, 0)),
            pl.BlockSpec((tm, mw), lambda i: (i, 0)),
        ],
        out_shape=[jax.ShapeDtypeStruct((t, d), BF16)] * 2
        + [jax.ShapeDtypeStruct((t // tm, d, tm), BF16), jax.ShapeDtypeStruct((t, mw), BF16)],
        compiler_params=pltpu.CompilerParams(
            dimension_semantics=("arbitrary",), vmem_limit_bytes=_vmem_limit(est)),
        name="in_proj",
    )(x2d, g.reshape(1, d), w)


def _conv_proj_kernel(x_ref, g_ref, w_ref, cw_ref, mix_ref, mq_ref, u_ref, *, tiles_per_seq):
    tm, d = x_ref.shape
    i = pl.program_id(0)

    @pl.when(i % tiles_per_seq == 0)
    def _():
        u_ref[0:SUBLANES, :] = jnp.zeros((SUBLANES, d), F32)

    hn = _rms(x_ref[...], g_ref[...], NORM_EPS).astype(BF16)
    gate_c = _dot(hn, w_ref[:, d:2 * d])
    h = _dot(hn, w_ref[:, 2 * d:3 * d])
    u_ref[SUBLANES:SUBLANES + tm, :] = gate_c * h
    y = cw_ref[CONV_WIDTH - 1:CONV_WIDTH, :] * u_ref[SUBLANES:SUBLANES + tm, :]
    for tap in range(CONV_WIDTH - 1):
        shift = CONV_WIDTH - 1 - tap
        y = y + cw_ref[tap:tap + 1, :] * u_ref[SUBLANES - shift:SUBLANES - shift + tm, :]
    gate_b = _dot(hn, w_ref[:, 0:d])
    mix_ref[...] = (gate_b * y).astype(BF16)
    mq_ref[...] = _dot(hn, w_ref[:, 3 * d:]).astype(BF16)
    u_ref[0:SUBLANES, :] = u_ref[tm:tm + SUBLANES, :]


def _conv_proj(x2d, g, w, conv_w, *, seq, tm):
    t, d = x2d.shape
    n = w.shape[1]
    mw = n - 3 * d
    est = 2 * tm * d * 4 + d * n * 2 + 2 * tm * (d + mw) * 2 + 5 * tm * d * 4
    return pl.pallas_call(
        functools.partial(_conv_proj_kernel, tiles_per_seq=seq // tm),
        grid=(t // tm,),
        in_specs=[
            pl.BlockSpec((tm, d), lambda i: (i, 0)),
            _const_spec((1, d)),
            _const_spec((d, n)),
            _const_spec((CONV_WIDTH, d)),
        ],
        out_specs=[
            pl.BlockSpec((tm, d), lambda i: (i, 0)),
            pl.BlockSpec((tm, mw), lambda i: (i, 0)),
        ],
        out_shape=[jax.ShapeDtypeStruct((t, d), BF16), jax.ShapeDtypeStruct((t, mw), BF16)],
        scratch_shapes=[pltpu.VMEM((tm + SUBLANES, d), F32)],
        compiler_params=pltpu.CompilerParams(
            dimension_semantics=("arbitrary",), vmem_limit_bytes=_vmem_limit(est)),
        name="conv_proj",
    )(x2d, g.reshape(1, d), w, conv_w)


def _col_reduce(x, op, reduce_fn):
    rows = x.shape[0]
    parts = [x[r * (rows // COL_REDUCE_WAYS):(r + 1) * (rows // COL_REDUCE_WAYS), :]
             for r in range(COL_REDUCE_WAYS)]
    while len(parts) > 1:
        parts = [op(parts[2 * r], parts[2 * r + 1]) for r in range(len(parts) // 2)]
    return reduce_fn(parts[0], axis=0, keepdims=True)


def _tile_validity(key_off, n_keys, qry_off, n_qry, strict):
    last_ok = key_off + n_keys - 1 < qry_off if strict else key_off + n_keys - 1 <= qry_off
    first_bad = key_off >= qry_off + n_qry - 1 if strict else key_off > qry_off + n_qry - 1
    return "all" if last_ok else ("none" if first_bad else "some")


def _sb_kernel(q_ref, k_ref, vt_ref, tri_ref, o_ref, run_ref, acc_ref, z_ref, *, tq, tk, qc, sub):
    i = pl.program_id(2)
    run_ref[...] = jnp.zeros_like(run_ref)
    acc_ref[...] = jnp.zeros_like(acc_ref)

    def load_kv(jb):
        return k_ref[0, pl.ds(pl.multiple_of(jb * tk, tk), tk), :], vt_ref[0, jb]

    def scores(c, ks):
        return jnp.minimum(_dot_nt(ks, q_ref[0, c * qc:(c + 1) * qc, :]), EXP2_ARG_MAX)

    def log_weights(c, z2, mask):
        cols = slice(c * qc, (c + 1) * qc)
        u = jnp.log2(1.0 + jnp.exp2(z2))
        if mask is not None:
            u = jnp.where(mask, u, 0.0)
        u_bf = u.astype(BF16)
        run = run_ref[:, cols]
        log_a = [None] * (tk // sub)
        for sb in reversed(range(tk // sub)):
            rows = slice(sb * sub, (sb + 1) * sub)
            suffix = _dot(tri_ref[...], u_bf[rows, :])
            log_a[sb] = z2[rows, :] - suffix - run
            run = run + suffix[0:1, :]
        run_ref[:, cols] = run
        return jnp.concatenate(log_a, axis=0)

    def accumulate(c, log_a, vt, mask):
        a = jnp.exp2(log_a)
        if mask is not None:
            a = jnp.where(mask, a, 0.0)
        acc_ref[:, c * qc:(c + 1) * qc] += _dot(vt, a.astype(BF16))

    assert (tq // tk) % 2 == 0
    n_before = i * (tq // tk)
    chunks = range(tq // qc)

    def prefetch_scores(jb, slot, active=chunks):
        ks, _ = load_kv(jnp.maximum(jb, 0))
        for c in active:
            z_ref[slot, c] = scores(c, ks)

    def process(jb, slot, active):
        _, vt = load_kv(jb)
        k_next, _ = load_kv(jnp.maximum(jb - 1, 0))
        for c in active:
            log_a = log_weights(c, z_ref[slot, c], None)
            z_ref[1 - slot, c] = scores(c, k_next)
            accumulate(c, log_a, vt, None)

    key = lax.broadcasted_iota(jnp.int32, (tk, qc), 0)
    qry = lax.broadcasted_iota(jnp.int32, (tk, qc), 1)
    diag = []
    for kb in reversed(range(tq // tk)):
        for c in chunks:
            validity = _tile_validity(kb * tk, tk, c * qc, qc, strict=True)
            if validity != "none":
                diag.append((c, kb, None if validity == "all" else key + kb * tk < qry + c * qc))
    kv = {kb: load_kv(i * (tq // tk) + kb) for kb in range(tq // tk)}
    z2 = [scores(c, kv[kb][0]) for c, kb, _ in diag]
    prefetch_scores(n_before - 1, 0)
    log_a = [log_weights(c, z2[n], mask) for n, (c, _, mask) in enumerate(diag)]
    for n, (c, kb, mask) in enumerate(diag):
        accumulate(c, log_a[n], kv[kb][1], mask)

    def weights_remain(min_run):
        return min_run <= EXP2_ARG_MAX - F32_EXP2_UNDERFLOW_ARG

    def visit_key_blocks(active):
        def min_run():
            return jnp.min(run_ref[:, active[0] * qc:(active[-1] + 1) * qc])

        def keep_going(carry):
            jj, least = carry
            return jnp.logical_and(jj < n_before // 2, weights_remain(least))

        def body(carry):
            jj, _ = carry
            jb = n_before - 1 - 2 * jj
            process(jb, 0, active)

            @pl.when(weights_remain(min_run()))
            def _():
                process(jb - 1, 1, active)

            return jj + 1, min_run()

        lax.while_loop(keep_going, body, (jnp.int32(0), min_run()))

    if len(chunks) > 1:
        later_remain = weights_remain(jnp.min(run_ref[:, qc:]))

        @pl.when(later_remain)
        def _():
            visit_key_blocks(tuple(chunks))

        @pl.when(jnp.logical_not(later_remain))
        def _():
            visit_key_blocks((0,))
    else:
        visit_key_blocks(tuple(chunks))

    o_ref[0] = acc_ref[...].T.astype(o_ref.dtype)


def _sb_attention(q, k, vt, *, tq):
    b, s, d = q.shape
    tk = vt.shape[3]
    heads = d // HEAD_DIM
    qc = min(tq, V7X_MXUS * V7X_MXU_DIM)
    sub = min(tk, V7X_MXU_DIM)
    r = jnp.arange(sub)
    tri = (r[None, :] >= r[:, None]).astype(BF16)
    est = 2 * s * HEAD_DIM * 2 + 12 * (tq // qc) * tk * qc * 4 + 4 * HEAD_DIM * tq * 4
    return pl.pallas_call(
        functools.partial(_sb_kernel, tq=tq, tk=tk, qc=qc, sub=sub),
        grid=(b, heads, s // tq),
        in_specs=[
            pl.BlockSpec((1, tq, HEAD_DIM), lambda bi, h, i: (bi, i, h)),
            pl.BlockSpec((1, s, HEAD_DIM), lambda bi, h, i: (bi, 0, h), pipeline_mode=pl.Buffered(1)),
            pl.BlockSpec((1, s // tk, HEAD_DIM, tk), lambda bi, h, i: (bi, 0, h, 0),
                         pipeline_mode=pl.Buffered(1)),
            _const_spec((sub, sub)),
        ],
        out_specs=pl.BlockSpec((1, tq, HEAD_DIM), lambda bi, h, i: (bi, i, h)),
        out_shape=jax.ShapeDtypeStruct((b, s, d), BF16),
        scratch_shapes=[pltpu.VMEM((1, tq), F32), pltpu.VMEM((HEAD_DIM, tq), F32),
                        pltpu.VMEM((2, tq // qc, tk, qc), F32)],
        compiler_params=pltpu.CompilerParams(
            dimension_semantics=("arbitrary", "arbitrary", "arbitrary"),
            vmem_limit_bytes=_vmem_limit(est)),
        name="sb_attention",
    )(q, k, vt, tri)


def _diff_kernel(slope2_ref, lq1_ref, lk1_ref, lq2_ref, lk2_ref, g_ref, qf_ref, kf_ref,
                 q1_ref, q2_ref, k1_ref, k2_ref, vt_ref, o_ref, m_ref, l_ref, acc_ref, s_ref,
                 kmax2_ref,
                 *, tq, tk, qc, lambda_init):
    h = pl.program_id(1)
    i = pl.program_id(2)
    slope2 = slope2_ref[h]
    m_ref[...] = jnp.full_like(m_ref, MASK_VALUE)
    l_ref[...] = jnp.zeros_like(l_ref)
    acc_ref[...] = jnp.zeros_like(acc_ref)
    qf = jnp.broadcast_to(qf_ref[0], (qc, qf_ref.shape[2]))
    q_refs = (q1_ref, q2_ref)
    k_refs = (k1_ref, k2_ref)

    def load_kv(jb):
        start = pl.multiple_of(jb * tk, tk)
        k_aug = [jnp.concatenate([k_refs[mp][0, pl.ds(start, tk), :], kf_ref[...]], axis=1)
                 for mp in range(2)]
        return k_aug, vt_ref[0, jb]

    def scores(mp, c, k_aug):
        q_aug = jnp.concatenate([q_refs[mp][0, c * qc:(c + 1) * qc, :], qf], axis=1)
        return _dot_nt(k_aug, q_aug)

    def probabilities(mp, c, s, off, mask):
        cols = slice(c * qc, (c + 1) * qc)
        if mask is not None:
            s = jnp.where(mask, s, MASK_VALUE)
        m = m_ref[mp, :, cols]
        m_new = jnp.maximum(m, _col_reduce(s, jnp.maximum, jnp.max) + off)
        alpha = jnp.exp2(m - m_new)
        p = jnp.exp2(s - (m_new - off))
        m_ref[mp, :, cols] = m_new
        l_ref[mp, :, cols] = alpha * l_ref[mp, :, cols] + _col_reduce(p, jnp.add, jnp.sum)
        return alpha, p.astype(BF16)

    def accumulate(mp, c, alpha, p_bf, vt):
        cols = slice(c * qc, (c + 1) * qc)
        acc_ref[mp, :, cols] = alpha * acc_ref[mp, :, cols] + _dot(vt, p_bf)

    assert (tq // tk) % 2 == 0
    n_before = i * (tq // tk)
    tiles = [(mp, c) for c in range(tq // qc) for mp in range(2)]

    def prefetch_scores(jb, slot):
        k_aug, _ = load_kv(jnp.maximum(jb, 0))
        for n, t in enumerate(tiles):
            s_ref[slot, n] = scores(*t, k_aug[t[0]])

    def process(jb, slot):
        _, vt = load_kv(jb)
        off = slope2 * (jb * tk - i * tq).astype(F32)
        k_next, _ = load_kv(jnp.maximum(jb - 1, 0))
        for n, t in enumerate(tiles):
            accumulate(*t, *probabilities(*t, s_ref[slot, n], off, None), vt)
            s_ref[1 - slot, n] = scores(*t, k_next[t[0]])

    key = lax.broadcasted_iota(jnp.int32, (tk, qc), 0)
    qry = lax.broadcasted_iota(jnp.int32, (tk, qc), 1)
    diag = []
    for kb in reversed(range(tq // tk)):
        for mp, c in tiles:
            validity = _tile_validity(kb * tk, tk, c * qc, qc, strict=False)
            if validity != "none":
                diag.append((mp, c, kb, None if validity == "all" else key + kb * tk <= qry + c * qc))
    kv = {kb: load_kv(i * (tq // tk) + kb) for kb in range(tq // tk)}
    s_diag = [scores(mp, c, kv[kb][0][mp]) for mp, c, kb, _ in diag]
    prefetch_scores(n_before - 1, 0)
    ap = [probabilities(mp, c, s_diag[n], slope2 * (kb * tk), mask)
          for n, (mp, c, kb, mask) in enumerate(diag)]
    for n, (mp, c, kb, _) in enumerate(diag):
        accumulate(mp, c, *ap[n], kv[kb][1])

    @pl.when(i == 0)
    def _():
        for mp in range(2):
            def norm_step(j, best):
                ks = k_refs[mp][0, pl.ds(pl.multiple_of(j * tk, tk), tk), :].astype(F32)
                best = jnp.maximum(best, jnp.max(jnp.sum(ks * ks, axis=1, keepdims=True)))
                kmax2_ref[mp, j] = best
                return best
            lax.fori_loop(0, k1_ref.shape[1] // tk, norm_step, jnp.float32(0.0))

    ones = jnp.ones((SUBLANES, q1_ref.shape[2]), BF16)
    qnorm2 = []
    for mp in range(2):
        q32 = q_refs[mp][0].astype(F32)
        qnorm2.append(_dot_nt(ones, (q32 * q32).astype(BF16))[0:1, :] * NORM_BOUND_MARGIN)

    def score_gap(jb):
        jb = jnp.maximum(jb, 0)
        gap = [jnp.max(jnp.sqrt(qnorm2[mp] * kmax2_ref[mp, jb]) - m_ref[mp]) for mp in range(2)]
        return (jnp.maximum(gap[0], gap[1]) + slope2 * (tk - 1)
                + slope2 * (jb * tk - i * tq).astype(F32))

    def keep_going(carry):
        jj, gap = carry
        return jnp.logical_and(jj < n_before // 2, gap >= F32_EXP2_UNDERFLOW_ARG - SCORE_BOUND_SLACK)

    def body(carry):
        jj, _ = carry
        jb = n_before - 1 - 2 * jj
        process(jb, 0)
        process(jb - 1, 1)
        return jj + 1, score_gap(jb - 2)

    lax.while_loop(keep_going, body, (jnp.int32(0), score_gap(n_before - 1)))

    lam = (jnp.exp(jnp.sum(lq1_ref[...] * lk1_ref[...], axis=1, keepdims=True))
           - jnp.exp(jnp.sum(lq2_ref[...] * lk2_ref[...], axis=1, keepdims=True))
           + lambda_init)
    o_t = acc_ref[0] / l_ref[0] - lam * (acc_ref[1] / l_ref[1])
    ms = jnp.mean(o_t * o_t, axis=0, keepdims=True)
    o_t = o_t * lax.rsqrt(ms + HEAD_NORM_EPS) * (g_ref[...] * (1.0 - lambda_init))
    o_ref[0] = o_t.T.astype(o_ref.dtype)


def _alibi_features(slope2, blk):
    pieces = []
    rest = slope2
    for _ in range(ALIBI_SLOPE_PIECES):
        piece = rest.astype(BF16)
        pieces.append(piece)
        rest = rest - piece.astype(F32)
    qf = jnp.stack([p * r for p in pieces for r in (ALIBI_POS_RADIX, 1)], axis=1)
    pos = jnp.arange(blk)
    digits = jnp.stack([pos // ALIBI_POS_RADIX, pos % ALIBI_POS_RADIX] * ALIBI_SLOPE_PIECES, axis=1)
    pad = HEAD_DIM - 2 * ALIBI_SLOPE_PIECES
    qf = jnp.pad(qf.astype(BF16), ((0, 0), (0, pad)))[:, None, :]
    kf = jnp.pad(digits.astype(BF16), ((0, 0), (0, pad)))
    return qf, kf


def _diff_attention(q, k, vt, lq1, lk1, lq2, lk2, g_head, *, layer_idx, tq):
    b, s, d = q.shape
    tk = vt.shape[3]
    heads = d // DIFF_V_DIM
    qc = min(tq, V7X_MXUS * V7X_MXU_DIM)
    assert tk <= ALIBI_POS_RADIX * 256
    lambda_init = 0.8 - 0.6 * math.exp(-0.3 * layer_idx)
    slope2 = LOG2_E * 2.0 ** (-8.0 * jnp.arange(1, heads + 1, dtype=F32) / heads)
    qf, kf = _alibi_features(slope2, tk)
    vec = lambda a: a.reshape(1, -1).astype(F32)
    est = (s * (2 * HEAD_DIM + DIFF_V_DIM) * 2 + 12 * 2 * (tq // qc) * tk * qc * 4
           + 4 * tq * DIFF_V_DIM * 4)
    return pl.pallas_call(
        functools.partial(_diff_kernel, tq=tq, tk=tk, qc=qc, lambda_init=lambda_init),
        grid=(b, heads, s // tq),
        in_specs=[
            pl.BlockSpec(memory_space=pltpu.SMEM),
            _const_spec((1, HEAD_DIM)), _const_spec((1, HEAD_DIM)),
            _const_spec((1, HEAD_DIM)), _const_spec((1, HEAD_DIM)),
            _const_spec((DIFF_V_DIM, 1)),
            pl.BlockSpec((1, 1, HEAD_DIM), lambda bi, h, i: (h, 0, 0)),
            _const_spec((tk, HEAD_DIM)),
            pl.BlockSpec((1, tq, HEAD_DIM), lambda bi, h, i: (bi, i, 2 * h)),
            pl.BlockSpec((1, tq, HEAD_DIM), lambda bi, h, i: (bi, i, 2 * h + 1)),
            pl.BlockSpec((1, s, HEAD_DIM), lambda bi, h, i: (bi, 0, 2 * h),
                         pipeline_mode=pl.Buffered(1)),
            pl.BlockSpec((1, s, HEAD_DIM), lambda bi, h, i: (bi, 0, 2 * h + 1),
                         pipeline_mode=pl.Buffered(1)),
            pl.BlockSpec((1, s // tk, DIFF_V_DIM, tk), lambda bi, h, i: (bi, 0, h, 0),
                         pipeline_mode=pl.Buffered(1)),
        ],
        out_specs=pl.BlockSpec((1, tq, DIFF_V_DIM), lambda bi, h, i: (bi, i, h)),
        out_shape=jax.ShapeDtypeStruct((b, s, d), BF16),
        scratch_shapes=[pltpu.VMEM((2, 1, tq), F32), pltpu.VMEM((2, 1, tq), F32),
                        pltpu.VMEM((2, DIFF_V_DIM, tq), F32),
                        pltpu.VMEM((2, 2 * (tq // qc), tk, qc), F32),
                        pltpu.SMEM((2, s // tk), F32)],
        compiler_params=pltpu.CompilerParams(
            dimension_semantics=("arbitrary", "arbitrary", "arbitrary"),
            vmem_limit_bytes=_vmem_limit(est)),
        name="diff_attention",
    )(slope2, vec(lq1), vec(lk1), vec(lq2), vec(lk2), g_head.reshape(-1, 1).astype(F32), qf, kf,
      q, q, k, k, vt)


def _post_kernel(mix_ref, mq_ref, x_ref, km_ref, vm_ref, wo_ref, g_ref, w1_ref, w2_ref, gf_ref,
                 o_ref, *, ff_chunks, final_norm):
    tm, d = x_ref.shape
    mw = mq_ref.shape[1]
    d_ff = w2_ref.shape[0]
    head_dim = mw // MEM_HEADS

    mq = mq_ref[...]
    km = km_ref[0]
    vm = vm_ref[0]
    lane_head = lax.broadcasted_iota(jnp.int32, (tm, mw), 1) // head_dim
    mo = jnp.zeros((tm, mw), F32)
    for hd in range(MEM_HEADS):
        in_head = lane_head == hd
        s = _dot_nt(jnp.where(in_head, mq, jnp.zeros_like(mq)), km)
        p = jnp.exp(s - jnp.max(s, axis=1, keepdims=True))
        l = jnp.sum(p, axis=1, keepdims=True)
        mo = jnp.where(in_head, _dot(p.astype(BF16), vm) / l, mo)

    y = x_ref[...] + _dot(mix_ref[...], wo_ref[0:d, :]) + _dot(mo.astype(BF16), wo_ref[d:, :])

    yn = _rms(y, g_ref[...], NORM_EPS).astype(BF16)
    hidden = []
    for lo, hi in ff_chunks:
        gate = _dot(yn, w1_ref[:, lo:hi])
        up = _dot(yn, w1_ref[:, d_ff + lo:d_ff + hi])
        hidden.append((gate * (1.0 / (1.0 + jnp.exp(-gate))) * up).astype(BF16))
    y = y + _dot(jnp.concatenate(hidden, axis=1), w2_ref[...])
    if final_norm:
        y = _rms(y, gf_ref[...], NORM_EPS)
    o_ref[...] = y


def _ff_chunks(d_ff, n_chunks, align):
    tiles = d_ff // align
    bounds = [align * ((tiles * c) // n_chunks) for c in range(n_chunks)] + [d_ff]
    return tuple((bounds[c], bounds[c + 1]) for c in range(n_chunks))


def _post(mix2d, mq2d, x2d, km, vm, wo, g_ffn, w1, w2, g_final, *, seq, tm, final_norm):
    t, d = x2d.shape
    mw = mq2d.shape[1]
    mem_len = km.shape[1]
    d_ff = w2.shape[0]
    tiles_per_seq = seq // tm
    ff_chunks = _ff_chunks(d_ff, 2, 256)
    chunk = max(hi - lo for lo, hi in ff_chunks)
    est = ((wo.size + w1.size + w2.size) * 2 + 2 * tm * (d + mw) * 2 + 4 * tm * d * 4
           + tm * chunk * 12 + 4 * tm * d * 4)
    return pl.pallas_call(
        functools.partial(_post_kernel, ff_chunks=ff_chunks, final_norm=final_norm),
        grid=(t // tm,),
        in_specs=[
            pl.BlockSpec((tm, d), lambda i: (i, 0)),
            pl.BlockSpec((tm, mw), lambda i: (i, 0)),
            pl.BlockSpec((tm, d), lambda i: (i, 0)),
            pl.BlockSpec((1, mem_len, mw), lambda i: (i // tiles_per_seq, 0, 0)),
            pl.BlockSpec((1, mem_len, mw), lambda i: (i // tiles_per_seq, 0, 0)),
            _const_spec(wo.shape),
            _const_spec((1, d)),
            _const_spec(w1.shape),
            _const_spec(w2.shape),
            _const_spec((1, d)),
        ],
        out_specs=pl.BlockSpec((tm, d), lambda i: (i, 0)),
        out_shape=jax.ShapeDtypeStruct((t, d), F32),
        compiler_params=pltpu.CompilerParams(
            dimension_semantics=("arbitrary",), vmem_limit_bytes=_vmem_limit(est)),
        name="post",
    )(mix2d, mq2d, x2d, km, vm, wo, g_ffn.reshape(1, d), w1, w2, g_final.reshape(1, d))


def kernel(x, mem, g_mix, w_in, w_mem_kv, w_o, g_ffn, w_ffn_in, w_ffn_out,
           lam_q1, lam_k1, lam_q2, lam_k2, g_diff_head, conv_w, g_mem, g_final):
    b, seq, d = x.shape
    mem_len = mem.shape[1]
    depth = w_in.shape[0]
    t = b * seq
    tm = min(512, seq)
    sb_tq = min(1024, seq)
    diff_tq = min(1024, seq)

    km_all, vm_all = _mem_kv(mem.reshape(b * mem_len, d), g_mem, w_mem_kv.astype(BF16))
    mw = km_all.shape[-1]
    km_all = km_all.reshape(depth, b, mem_len, mw)
    vm_all = vm_all.reshape(depth, b, mem_len, mw)

    x2d = x.reshape(t, d)
    for i in range(depth):
        kind = i % N_MIXERS
        j = i // N_MIXERS
        w = w_in[i].astype(BF16)
        if kind == 2:
            mix, mq = _conv_proj(x2d, g_mix[i], w, conv_w[j], seq=seq, tm=tm)
        else:
            q, k, vt, mq = _in_proj(x2d, g_mix[i], w, q_scale=LOG2_E * HEAD_DIM ** -0.5, tm=tm)
            q, k = (a.reshape(b, seq, d) for a in (q, k))
            vt = vt.reshape(b, seq // tm, d, tm)
            if kind == 0:
                mix = _sb_attention(q, k, vt, tq=sb_tq)
            else:
                mix = _diff_attention(q, k, vt, lam_q1[j], lam_k1[j], lam_q2[j], lam_k2[j],
                                      g_diff_head[j], layer_idx=i, tq=diff_tq)
            mix = mix.reshape(t, d)
        x2d = _post(mix, mq, x2d, km_all[i], vm_all[i], w_o[i].astype(BF16), g_ffn[i],
                    w_ffn_in[i].astype(BF16), w_ffn_out[i].astype(BF16), g_final,
                    seq=seq, tm=tm, final_norm=(i == depth - 1))
    return x2d.reshape(b, seq, d)
```

```python
import functools
import math

import jax
import jax.numpy as jnp
from jax import lax
from jax.experimental import pallas as pl
from jax.experimental.pallas import tpu as pltpu

F32 = jnp.float32
BF16 = jnp.bfloat16

N_MIXERS = 3
HEAD_DIM = 128
DIFF_V_DIM = 2 * HEAD_DIM
MEM_HEADS = 4
CONV_WIDTH = 3
NORM_EPS = 1e-6
HEAD_NORM_EPS = 1e-5
MASK_VALUE = -1e30
EXP2_ARG_MAX = 126.0
F32_EXP2_UNDERFLOW_ARG = -150.0
NORM_BOUND_MARGIN = 1.0 + 2.0 ** -6
SCORE_BOUND_SLACK = 2.0
ALIBI_SLOPE_PIECES = 3
ALIBI_POS_RADIX = 16

V7X_VMEM_BYTES = 64 * 1024 * 1024
V7X_MXU_DIM = 256
V7X_MXUS = 2
LOG2_E = 1.4426950408889634
SUBLANES = 8
COL_REDUCE_WAYS = 8


def _vmem_limit(estimate_bytes):
    return int(min(max(estimate_bytes, 16 * 1024 * 1024), V7X_VMEM_BYTES - 8 * 1024 * 1024))


def _rms(xf, g, eps):
    ms = jnp.mean(xf * xf, axis=-1, keepdims=True)
    return xf * lax.rsqrt(ms + eps) * g


def _dot(a, b):
    return jnp.dot(a, b, preferred_element_type=F32)


def _dot_nt(a, b):
    return lax.dot_general(a, b, (((1,), (1,)), ((), ())), preferred_element_type=F32)


def _const_spec(shape):
    nd = len(shape)
    return pl.BlockSpec(shape, lambda *_: (0,) * nd, pipeline_mode=pl.Buffered(1))


def _mem_kv_kernel(mem_ref, g_ref, w_ref, k_ref, v_ref, *, mem_width):
    mem_n = _rms(mem_ref[...], g_ref[...], NORM_EPS).astype(BF16)
    kv = _dot(mem_n, w_ref[0])
    head_dim = mem_width // MEM_HEADS
    k_ref[0] = (kv[:, :mem_width] * head_dim ** -0.5).astype(BF16)
    v_ref[0] = kv[:, mem_width:].astype(BF16)


def _mem_kv(mem2d, g_mem, w_kv):
    depth, d_model, two_w = w_kv.shape
    mem_width = two_w // 2
    rows = mem2d.shape[0]
    return pl.pallas_call(
        functools.partial(_mem_kv_kernel, mem_width=mem_width),
        grid=(depth,),
        in_specs=[
            pl.BlockSpec((rows, d_model), lambda i: (0, 0)),
            pl.BlockSpec((1, d_model), lambda i: (0, 0)),
            pl.BlockSpec((1, d_model, two_w), lambda i: (i, 0, 0)),
        ],
        out_specs=[
            pl.BlockSpec((1, rows, mem_width), lambda i: (i, 0, 0)),
            pl.BlockSpec((1, rows, mem_width), lambda i: (i, 0, 0)),
        ],
        out_shape=[jax.ShapeDtypeStruct((depth, rows, mem_width), BF16)] * 2,
        name="mem_kv",
    )(mem2d, g_mem.reshape(1, d_model), w_kv)


def _in_proj_kernel(x_ref, g_ref, w_ref, q_ref, k_ref, vt_ref, mq_ref, *, q_scale):
    d = x_ref.shape[1]
    hn = _rms(x_ref[...], g_ref[...], NORM_EPS).astype(BF16)
    q_ref[...] = (_dot(hn, w_ref[:, 0:d]) * q_scale).astype(BF16)
    k_ref[...] = _dot(hn, w_ref[:, d:2 * d]).astype(BF16)
    vt_ref[0] = _dot(hn, w_ref[:, 2 * d:3 * d]).T.astype(BF16)
    mq_ref[...] = _dot(hn, w_ref[:, 3 * d:]).astype(BF16)


def _in_proj(x2d, g, w, *, q_scale, tm):
    t, d = x2d.shape
    n = w.shape[1]
    mw = n - 3 * d
    est = 2 * tm * d * 4 + d * n * 2 + 2 * tm * n * 2 + 3 * tm * d * 4
    return pl.pallas_call(
        functools.partial(_in_proj_kernel, q_scale=q_scale),
        grid=(t // tm,),
        in_specs=[
            pl.BlockSpec((tm, d), lambda i: (i, 0)),
            _const_spec((1, d)),
            _const_spec((d, n)),
        ],
        out_specs=[
            pl.BlockSpec((tm, d), lambda i: (i, 0)),
            pl.BlockSpec((tm, d), lambda i: (i, 0)),
            pl.BlockSpec((1, d, tm), lambda i: (i, 0, 0)),
            pl.BlockSpec((tm, mw), lambda i: (i, 0)),
        ],
        out_shape=[jax.ShapeDtypeStruct((t, d), BF16)] * 2
        + [jax.ShapeDtypeStruct((t // tm, d, tm), BF16), jax.ShapeDtypeStruct((t, mw), BF16)],
        compiler_params=pltpu.CompilerParams(
            dimension_semantics=("arbitrary",), vmem_limit_bytes=_vmem_limit(est)),
        name="in_proj",
    )(x2d, g.reshape(1, d), w)


def _conv_proj_kernel(x_ref, g_ref, w_ref, cw_ref, mix_ref, mq_ref, u_ref, *, tiles_per_seq):
    tm, d = x_ref.shape
    i = pl.program_id(0)

    @pl.when(i % tiles_per_seq == 0)
    def _():
        u_ref[0:SUBLANES, :] = jnp.zeros((SUBLANES, d), F32)

    hn = _rms(x_ref[...], g_ref[...], NORM_EPS).astype(BF16)
    gate_c = _dot(hn, w_ref[:, d:2 * d])
    h = _dot(hn, w_ref[:, 2 * d:3 * d])
    u_ref[SUBLANES:SUBLANES + tm, :] = gate_c * h
    y = cw_ref[CONV_WIDTH - 1:CONV_WIDTH, :] * u_ref[SUBLANES:SUBLANES + tm, :]
    for tap in range(CONV_WIDTH - 1):
        shift = CONV_WIDTH - 1 - tap
        y = y + cw_ref[tap:tap + 1, :] * u_ref[SUBLANES - shift:SUBLANES - shift + tm, :]
    gate_b = _dot(hn, w_ref[:, 0:d])
    mix_ref[...] = (gate_b * y).astype(BF16)
    mq_ref[...] = _dot(hn, w_ref[:, 3 * d:]).astype(BF16)
    u_ref[0:SUBLANES, :] = u_ref[tm:tm + SUBLANES, :]


def _conv_proj(x2d, g, w, conv_w, *, seq, tm):
    t, d = x2d.shape
    n = w.shape[1]
    mw = n - 3 * d
    est = 2 * tm * d * 4 + d * n * 2 + 2 * tm * (d + mw) * 2 + 5 * tm * d * 4
    return pl.pallas_call(
        functools.partial(_conv_proj_kernel, tiles_per_seq=seq // tm),
        grid=(t // tm,),
        in_specs=[
            pl.BlockSpec((tm, d), lambda i: (i, 0)),
            _const_spec((1, d)),
            _const_spec((d, n)),
            _const_spec((CONV_WIDTH, d)),
        ],
        out_specs=[
            pl.BlockSpec((tm, d), lambda i: (i, 0)),
            pl.BlockSpec((tm, mw), lambda i: (i, 0)),
        ],
        out_shape=[jax.ShapeDtypeStruct((t, d), BF16), jax.ShapeDtypeStruct((t, mw), BF16)],
        scratch_shapes=[pltpu.VMEM((tm + SUBLANES, d), F32)],
        compiler_params=pltpu.CompilerParams(
            dimension_semantics=("arbitrary",), vmem_limit_bytes=_vmem_limit(est)),
        name="conv_proj",
    )(x2d, g.reshape(1, d), w, conv_w)


def _col_reduce(x, op, reduce_fn):
    rows = x.shape[0]
    parts = [x[r * (rows // COL_REDUCE_WAYS):(r + 1) * (rows // COL_REDUCE_WAYS), :]
             for r in range(COL_REDUCE_WAYS)]
    while len(parts) > 1:
        parts = [op(parts[2 * r], parts[2 * r + 1]) for r in range(len(parts) // 2)]
    return reduce_fn(parts[0], axis=0, keepdims=True)


def _tile_validity(key_off, n_keys, qry_off, n_qry, strict):
    last_ok = key_off + n_keys - 1 < qry_off if strict else key_off + n_keys - 1 <= qry_off
    first_bad = key_off >= qry_off + n_qry - 1 if strict else key_off > qry_off + n_qry - 1
    return "all" if last_ok else ("none" if first_bad else "some")


def _sb_kernel(q_ref, k_ref, vt_ref, tri_ref, o_ref, run_ref, acc_ref, z_ref, *, tq, tk, qc, sub):
    i = pl.program_id(2)
    run_ref[...] = jnp.zeros_like(run_ref)
    acc_ref[...] = jnp.zeros_like(acc_ref)

    def load_kv(jb):
        return k_ref[0, pl.ds(pl.multiple_of(jb * tk, tk), tk), :], vt_ref[0, jb]

    def scores(c, ks):
        return jnp.minimum(_dot_nt(ks, q_ref[0, c * qc:(c + 1) * qc, :]), EXP2_ARG_MAX)

    def log_weights(c, z2, mask):
        cols = slice(c * qc, (c + 1) * qc)
        u = jnp.log2(1.0 + jnp.exp2(z2))
        if mask is not None:
            u = jnp.where(mask, u, 0.0)
        u_bf = u.astype(BF16)
        run = run_ref[:, cols]
        log_a = [None] * (tk // sub)
        for sb in reversed(range(tk // sub)):
            rows = slice(sb * sub, (sb + 1) * sub)
            suffix = _dot(tri_ref[...], u_bf[rows, :])
            log_a[sb] = z2[rows, :] - suffix - run
            run = run + suffix[0:1, :]
        run_ref[:, cols] = run
        return jnp.concatenate(log_a, axis=0)

    def accumulate(c, log_a, vt, mask):
        a = jnp.exp2(log_a)
        if mask is not None:
            a = jnp.where(mask, a, 0.0)
        acc_ref[:, c * qc:(c + 1) * qc] += _dot(vt, a.astype(BF16))

    assert (tq // tk) % 2 == 0
    n_before = i * (tq // tk)
    chunks = range(tq // qc)

    def prefetch_scores(jb, slot, active=chunks):
        ks, _ = load_kv(jnp.maximum(jb, 0))
        for c in active:
            z_ref[slot, c] = scores(c, ks)

    def process(jb, slot, active):
        _, vt = load_kv(jb)
        k_next, _ = load_kv(jnp.maximum(jb - 1, 0))
        for c in active:
            log_a = log_weights(c, z_ref[slot, c], None)
            z_ref[1 - slot, c] = scores(c, k_next)
            accumulate(c, log_a, vt, None)

    key = lax.broadcasted_iota(jnp.int32, (tk, qc), 0)
    qry = lax.broadcasted_iota(jnp.int32, (tk, qc), 1)
    diag = []
    for kb in reversed(range(tq // tk)):
        for c in chunks:
            validity = _tile_validity(kb * tk, tk, c * qc, qc, strict=True)
            if validity != "none":
                diag.append((c, kb, None if validity == "all" else key + kb * tk < qry + c * qc))
    kv = {kb: load_kv(i * (tq // tk) + kb) for kb in range(tq // tk)}
    first = min(2, len(diag))
    z2 = [scores(c, kv[kb][0]) for c, kb, _ in diag[:first]]
    log_a = [log_weights(diag[0][0], z2[0], diag[0][2])]
    z2 += [scores(c, kv[kb][0]) for c, kb, _ in diag[first:]]
    prefetch_scores(n_before - 1, 0)
    log_a += [log_weights(c, z2[n], mask) for n, (c, _, mask) in enumerate(diag) if n > 0]
    for n, (c, kb, mask) in enumerate(diag):
        accumulate(c, log_a[n], kv[kb][1], mask)

    def weights_remain(min_run):
        return min_run <= EXP2_ARG_MAX - F32_EXP2_UNDERFLOW_ARG

    def visit_key_blocks(active):
        def min_run():
            return jnp.min(run_ref[:, active[0] * qc:(active[-1] + 1) * qc])

        def keep_going(carry):
            jj, least = carry
            return jnp.logical_and(jj < n_before // 2, weights_remain(least))

        def body(carry):
            jj, _ = carry
            jb = n_before - 1 - 2 * jj
            process(jb, 0, active)

            @pl.when(weights_remain(min_run()))
            def _():
                process(jb - 1, 1, active)

            return jj + 1, min_run()

        lax.while_loop(keep_going, body, (jnp.int32(0), min_run()))

    if len(chunks) > 1:
        later_remain = weights_remain(jnp.min(run_ref[:, qc:]))

        @pl.when(later_remain)
        def _():
            visit_key_blocks(tuple(chunks))

        @pl.when(jnp.logical_not(later_remain))
        def _():
            visit_key_blocks((0,))
    else:
        visit_key_blocks(tuple(chunks))

    o_ref[0] = acc_ref[...].T.astype(o_ref.dtype)


def _sb_attention(q, k, vt, *, tq):
    b, s, d = q.shape
    tk = vt.shape[3]
    heads = d // HEAD_DIM
    qc = min(tq, V7X_MXUS * V7X_MXU_DIM)
    sub = min(tk, V7X_MXU_DIM)
    r = jnp.arange(sub)
    tri = (r[None, :] >= r[:, None]).astype(BF16)
    est = 2 * 2 * s * HEAD_DIM * 2 + 12 * (tq // qc) * tk * qc * 4 + 4 * HEAD_DIM * tq * 4
    return pl.pallas_call(
        functools.partial(_sb_kernel, tq=tq, tk=tk, qc=qc, sub=sub),
        grid=(b, heads, s // tq),
        in_specs=[
            pl.BlockSpec((1, tq, HEAD_DIM), lambda bi, h, i: (bi, i, h)),
            pl.BlockSpec((1, s, HEAD_DIM), lambda bi, h, i: (bi, 0, h)),
            pl.BlockSpec((1, s // tk, HEAD_DIM, tk), lambda bi, h, i: (bi, 0, h, 0)),
            _const_spec((sub, sub)),
        ],
        out_specs=pl.BlockSpec((1, tq, HEAD_DIM), lambda bi, h, i: (bi, i, h)),
        out_shape=jax.ShapeDtypeStruct((b, s, d), BF16),
        scratch_shapes=[pltpu.VMEM((1, tq), F32), pltpu.VMEM((HEAD_DIM, tq), F32),
                        pltpu.VMEM((2, tq // qc, tk, qc), F32)],
        compiler_params=pltpu.CompilerParams(
            dimension_semantics=("arbitrary", "arbitrary", "arbitrary"),
            vmem_limit_bytes=_vmem_limit(est)),
        name="sb_attention",
    )(q, k, vt, tri)


def _diff_kernel(slope2_ref, lq1_ref, lk1_ref, lq2_ref, lk2_ref, g_ref, qf_ref, kf_ref,
                 q1_ref, q2_ref, k1_ref, k2_ref, vt_ref, o_ref, m_ref, l_ref, acc_ref, s_ref,
                 kmax2_ref,
                 *, tq, tk, qc, lambda_init):
    h = pl.program_id(1)
    i = pl.program_id(2)
    slope2 = slope2_ref[h]
    m_ref[...] = jnp.full_like(m_ref, MASK_VALUE)
    l_ref[...] = jnp.zeros_like(l_ref)
    acc_ref[...] = jnp.zeros_like(acc_ref)
    qf = jnp.broadcast_to(qf_ref[0], (qc, qf_ref.shape[2]))
    q_refs = (q1_ref, q2_ref)
    k_refs = (k1_ref, k2_ref)

    def load_kv(jb):
        start = pl.multiple_of(jb * tk, tk)
        k_aug = [jnp.concatenate([k_refs[mp][0, pl.ds(start, tk), :], kf_ref[...]], axis=1)
                 for mp in range(2)]
        return k_aug, vt_ref[0, jb]

    def scores(mp, c, k_aug):
        q_aug = jnp.concatenate([q_refs[mp][0, c * qc:(c + 1) * qc, :], qf], axis=1)
        return _dot_nt(k_aug, q_aug)

    def probabilities(mp, c, s, off, mask):
        cols = slice(c * qc, (c + 1) * qc)
        if mask is not None:
            s = jnp.where(mask, s, MASK_VALUE)
        m = m_ref[mp, :, cols]
        m_new = jnp.maximum(m, _col_reduce(s, jnp.maximum, jnp.max) + off)
        alpha = jnp.exp2(m - m_new)
        p = jnp.exp2(s - (m_new - off))
        m_ref[mp, :, cols] = m_new
        l_ref[mp, :, cols] = alpha * l_ref[mp, :, cols] + _col_reduce(p, jnp.add, jnp.sum)
        return alpha, p.astype(BF16)

    def accumulate(mp, c, alpha, p_bf, vt):
        cols = slice(c * qc, (c + 1) * qc)
        acc_ref[mp, :, cols] = alpha * acc_ref[mp, :, cols] + _dot(vt, p_bf)

    assert (tq // tk) % 2 == 0
    n_before = i * (tq // tk)
    tiles = [(mp, c) for c in range(tq // qc) for mp in range(2)]

    def prefetch_scores(jb, slot):
        k_aug, _ = load_kv(jnp.maximum(jb, 0))
        for n, t in enumerate(tiles):
            s_ref[slot, n] = scores(*t, k_aug[t[0]])

    def process(jb, slot):
        _, vt = load_kv(jb)
        off = slope2 * (jb * tk - i * tq).astype(F32)
        k_next, _ = load_kv(jnp.maximum(jb - 1, 0))
        for n, t in enumerate(tiles):
            accumulate(*t, *probabilities(*t, s_ref[slot, n], off, None), vt)
            s_ref[1 - slot, n] = scores(*t, k_next[t[0]])

    key = lax.broadcasted_iota(jnp.int32, (tk, qc), 0)
    qry = lax.broadcasted_iota(jnp.int32, (tk, qc), 1)
    diag = []
    for kb in reversed(range(tq // tk)):
        for mp, c in tiles:
            validity = _tile_validity(kb * tk, tk, c * qc, qc, strict=False)
            if validity != "none":
                diag.append((mp, c, kb, None if validity == "all" else key + kb * tk <= qry + c * qc))
    kv = {kb: load_kv(i * (tq // tk) + kb) for kb in range(tq // tk)}
    s_diag = [scores(mp, c, kv[kb][0][mp]) for mp, c, kb, _ in diag]
    prefetch_scores(n_before - 1, 0)
    ap = [probabilities(mp, c, s_diag[n], slope2 * (kb * tk), mask)
          for n, (mp, c, kb, mask) in enumerate(diag)]
    for n, (mp, c, kb, _) in enumerate(diag):
        accumulate(mp, c, *ap[n], kv[kb][1])

    @pl.when(i == 0)
    def _():
        for mp in range(2):
            def norm_step(j, best):
                ks = k_refs[mp][0, pl.ds(pl.multiple_of(j * tk, tk), tk), :].astype(F32)
                best = jnp.maximum(best, jnp.max(jnp.sum(ks * ks, axis=1, keepdims=True)))
                kmax2_ref[mp, j] = best
                return best
            lax.fori_loop(0, k1_ref.shape[1] // tk, norm_step, jnp.float32(0.0))

    ones = jnp.ones((SUBLANES, q1_ref.shape[2]), BF16)
    qnorm2 = []
    for mp in range(2):
        q32 = q_refs[mp][0].astype(F32)
        qnorm2.append(_dot_nt(ones, (q32 * q32).astype(BF16))[0:1, :] * NORM_BOUND_MARGIN)

    def score_gap(jb):
        jb = jnp.maximum(jb, 0)
        gap = [jnp.max(jnp.sqrt(qnorm2[mp] * kmax2_ref[mp, jb]) - m_ref[mp]) for mp in range(2)]
        return (jnp.maximum(gap[0], gap[1]) + slope2 * (tk - 1)
                + slope2 * (jb * tk - i * tq).astype(F32))

    def keep_going(carry):
        jj, gap = carry
        return jnp.logical_and(jj < n_before // 2, gap >= F32_EXP2_UNDERFLOW_ARG - SCORE_BOUND_SLACK)

    def body(carry):
        jj, _ = carry
        jb = n_before - 1 - 2 * jj
        process(jb, 0)
        process(jb - 1, 1)
        return jj + 1, score_gap(jb - 2)

    lax.while_loop(keep_going, body, (jnp.int32(0), score_gap(n_before - 1)))

    lam = (jnp.exp(jnp.sum(lq1_ref[...] * lk1_ref[...], axis=1, keepdims=True))
           - jnp.exp(jnp.sum(lq2_ref[...] * lk2_ref[...], axis=1, keepdims=True))
           + lambda_init)
    o_t = acc_ref[0] / l_ref[0] - lam * (acc_ref[1] / l_ref[1])
    ms = jnp.mean(o_t * o_t, axis=0, keepdims=True)
    o_t = o_t * lax.rsqrt(ms + HEAD_NORM_EPS) * (g_ref[...] * (1.0 - lambda_init))
    o_ref[0] = o_t.T.astype(o_ref.dtype)


def _alibi_features(slope2, blk):
    pieces = []
    rest = slope2
    for _ in range(ALIBI_SLOPE_PIECES):
        piece = rest.astype(BF16)
        pieces.append(piece)
        rest = rest - piece.astype(F32)
    qf = jnp.stack([p * r for p in pieces for r in (ALIBI_POS_RADIX, 1)], axis=1)
    pos = jnp.arange(blk)
    digits = jnp.stack([pos // ALIBI_POS_RADIX, pos % ALIBI_POS_RADIX] * ALIBI_SLOPE_PIECES, axis=1)
    pad = HEAD_DIM - 2 * ALIBI_SLOPE_PIECES
    qf = jnp.pad(qf.astype(BF16), ((0, 0), (0, pad)))[:, None, :]
    kf = jnp.pad(digits.astype(BF16), ((0, 0), (0, pad)))
    return qf, kf


def _diff_attention(q, k, vt, lq1, lk1, lq2, lk2, g_head, *, layer_idx, tq):
    b, s, d = q.shape
    tk = vt.shape[3]
    heads = d // DIFF_V_DIM
    qc = min(tq, V7X_MXUS * V7X_MXU_DIM)
    assert tk <= ALIBI_POS_RADIX * 256
    lambda_init = 0.8 - 0.6 * math.exp(-0.3 * layer_idx)
    slope2 = LOG2_E * 2.0 ** (-8.0 * jnp.arange(1, heads + 1, dtype=F32) / heads)
    qf, kf = _alibi_features(slope2, tk)
    vec = lambda a: a.reshape(1, -1).astype(F32)
    est = (s * (2 * HEAD_DIM + DIFF_V_DIM) * 2 + 12 * 2 * (tq // qc) * tk * qc * 4
           + 4 * tq * DIFF_V_DIM * 4)
    return pl.pallas_call(
        functools.partial(_diff_kernel, tq=tq, tk=tk, qc=qc, lambda_init=lambda_init),
        grid=(b, heads, s // tq),
        in_specs=[
            pl.BlockSpec(memory_space=pltpu.SMEM),
            _const_spec((1, HEAD_DIM)), _const_spec((1, HEAD_DIM)),
            _const_spec((1, HEAD_DIM)), _const_spec((1, HEAD_DIM)),
            _const_spec((DIFF_V_DIM, 1)),
            pl.BlockSpec((1, 1, HEAD_DIM), lambda bi, h, i: (h, 0, 0)),
            _const_spec((tk, HEAD_DIM)),
            pl.BlockSpec((1, tq, HEAD_DIM), lambda bi, h, i: (bi, i, 2 * h)),
            pl.BlockSpec((1, tq, HEAD_DIM), lambda bi, h, i: (bi, i, 2 * h + 1)),
            pl.BlockSpec((1, s, HEAD_DIM), lambda bi, h, i: (bi, 0, 2 * h),
                         pipeline_mode=pl.Buffered(1)),
            pl.BlockSpec((1, s, HEAD_DIM), lambda bi, h, i: (bi, 0, 2 * h + 1),
                         pipeline_mode=pl.Buffered(1)),
            pl.BlockSpec((1, s // tk, DIFF_V_DIM, tk), lambda bi, h, i: (bi, 0, h, 0),
                         pipeline_mode=pl.Buffered(1)),
        ],
        out_specs=pl.BlockSpec((1, tq, DIFF_V_DIM), lambda bi, h, i: (bi, i, h)),
        out_shape=jax.ShapeDtypeStruct((b, s, d), BF16),
        scratch_shapes=[pltpu.VMEM((2, 1, tq), F32), pltpu.VMEM((2, 1, tq), F32),
                        pltpu.VMEM((2, DIFF_V_DIM, tq), F32),
                        pltpu.VMEM((2, 2 * (tq // qc), tk, qc), F32),
                        pltpu.SMEM((2, s // tk), F32)],
        compiler_params=pltpu.CompilerParams(
            dimension_semantics=("arbitrary", "arbitrary", "arbitrary"),
            vmem_limit_bytes=_vmem_limit(est)),
        name="diff_attention",
    )(slope2, vec(lq1), vec(lk1), vec(lq2), vec(lk2), g_head.reshape(-1, 1).astype(F32), qf, kf,
      q, q, k, k, vt)


def _post_kernel(mix_ref, mq_ref, x_ref, km_ref, vm_ref, wo_ref, g_ref, w1_ref, w2_ref, gf_ref,
                 o_ref, *, ff_chunks, final_norm):
    tm, d = x_ref.shape
    mw = mq_ref.shape[1]
    d_ff = w2_ref.shape[0]
    head_dim = mw // MEM_HEADS

    mq = mq_ref[...]
    km = km_ref[0]
    vm = vm_ref[0]
    lane_head = lax.broadcasted_iota(jnp.int32, (tm, mw), 1) // head_dim
    mo = jnp.zeros((tm, mw), F32)
    for hd in range(MEM_HEADS):
        in_head = lane_head == hd
        s = _dot_nt(jnp.where(in_head, mq, jnp.zeros_like(mq)), km)
        p = jnp.exp(s - jnp.max(s, axis=1, keepdims=True))
        l = jnp.sum(p, axis=1, keepdims=True)
        mo = jnp.where(in_head, _dot(p.astype(BF16), vm) / l, mo)

    y = x_ref[...] + _dot(mix_ref[...], wo_ref[0:d, :]) + _dot(mo.astype(BF16), wo_ref[d:, :])

    yn = _rms(y, g_ref[...], NORM_EPS).astype(BF16)
    hidden = []
    for lo, hi in ff_chunks:
        gate = _dot(yn, w1_ref[:, lo:hi])
        up = _dot(yn, w1_ref[:, d_ff + lo:d_ff + hi])
        hidden.append((gate * (1.0 / (1.0 + jnp.exp(-gate))) * up).astype(BF16))
    y = y + _dot(jnp.concatenate(hidden, axis=1), w2_ref[...])
    if final_norm:
        y = _rms(y, gf_ref[...], NORM_EPS)
    o_ref[...] = y


def _ff_chunks(d_ff, n_chunks, align):
    tiles = d_ff // align
    bounds = [align * ((tiles * c) // n_chunks) for c in range(n_chunks)] + [d_ff]
    return tuple((bounds[c], bounds[c + 1]) for c in range(n_chunks))


def _post(mix2d, mq2d, x2d, km, vm, wo, g_ffn, w1, w2, g_final, *, seq, tm, final_norm):
    t, d = x2d.shape
    mw = mq2d.shape[1]
    mem_len = km.shape[1]
    d_ff = w2.shape[0]
    tiles_per_seq = seq // tm
    ff_chunks = _ff_chunks(d_ff, 2, 256)
    chunk = max(hi - lo for lo, hi in ff_chunks)
    est = ((wo.size + w1.size + w2.size) * 2 + 2 * tm * (d + mw) * 2 + 4 * tm * d * 4
           + tm * chunk * 12 + 4 * tm * d * 4)
    return pl.pallas_call(
        functools.partial(_post_kernel, ff_chunks=ff_chunks, final_norm=final_norm),
        grid=(t // tm,),
        in_specs=[
            pl.BlockSpec((tm, d), lambda i: (i, 0)),
            pl.BlockSpec((tm, mw), lambda i: (i, 0)),
            pl.BlockSpec((tm, d), lambda i: (i, 0)),
            pl.BlockSpec((1, mem_len, mw), lambda i: (i // tiles_per_seq, 0, 0)),
            pl.BlockSpec((1, mem_len, mw), lambda i: (i // tiles_per_seq, 0, 0)),
            _const_spec(wo.shape),
            _const_spec((1, d)),
            _const_spec(w1.shape),
            _const_spec(w2.shape),
            _const_spec((1, d)),
        ],
        out_specs=pl.BlockSpec((tm, d), lambda i: (i, 0)),
        out_shape=jax.ShapeDtypeStruct((t, d), F32),
        compiler_params=pltpu.CompilerParams(
            dimension_semantics=("arbitrary",), vmem_limit_bytes=_vmem_limit(est)),
        name="post",
    )(mix2d, mq2d, x2d, km, vm, wo, g_ffn.reshape(1, d), w1, w2, g_final.reshape(1, d))


def kernel(x, mem, g_mix, w_in, w_mem_kv, w_o, g_ffn, w_ffn_in, w_ffn_out,
           lam_q1, lam_k1, lam_q2, lam_k2, g_diff_head, conv_w, g_mem, g_final):
    b, seq, d = x.shape
    mem_len = mem.shape[1]
    depth = w_in.shape[0]
    t = b * seq
    tm = min(512, seq)
    sb_tq = min(1024, seq)
    diff_tq = min(1024, seq)

    km_all, vm_all = _mem_kv(mem.reshape(b * mem_len, d), g_mem, w_mem_kv.astype(BF16))
    mw = km_all.shape[-1]
    km_all = km_all.reshape(depth, b, mem_len, mw)
    vm_all = vm_all.reshape(depth, b, mem_len, mw)

    x2d = x.reshape(t, d)
    for i in range(depth):
        kind = i % N_MIXERS
        j = i // N_MIXERS
        w = w_in[i].astype(BF16)
        if kind == 2:
            mix, mq = _conv_proj(x2d, g_mix[i], w, conv_w[j], seq=seq, tm=tm)
        else:
            q, k, vt, mq = _in_proj(x2d, g_mix[i], w, q_scale=LOG2_E * HEAD_DIM ** -0.5, tm=tm)
            q, k = (a.reshape(b, seq, d) for a in (q, k))
            vt = vt.reshape(b, seq // tm, d, tm)
            if kind == 0:
                mix = _sb_attention(q, k, vt, tq=sb_tq)
            else:
                mix = _diff_attention(q, k, vt, lam_q1[j], lam_k1[j], lam_q2[j], lam_k2[j],
                                      g_diff_head[j], layer_idx=i, tq=diff_tq)
            mix = mix.reshape(t, d)
        x2d = _post(mix, mq, x2d, km_all[i], vm_all[i], w_o[i].astype(BF16), g_ffn[i],
                    w_ffn_in[i].astype(BF16), w_ffn_out[i].astype(BF16), g_final,
                    seq=seq, tm=tm, final_norm=(i == depth - 1))
    return x2d.reshape(b, seq, d)
```

```python
import functools
import math

import jax
import jax.numpy as jnp
from jax import lax
from jax.experimental import pallas as pl
from jax.experimental.pallas import tpu as pltpu

F32 = jnp.float32
BF16 = jnp.bfloat16

N_MIXERS = 3
HEAD_DIM = 128
DIFF_V_DIM = 2 * HEAD_DIM
MEM_HEADS = 4
CONV_WIDTH = 3
NORM_EPS = 1e-6
HEAD_NORM_EPS = 1e-5
MASK_VALUE = -1e30
EXP2_ARG_MAX = 126.0
F32_EXP2_UNDERFLOW_ARG = -150.0
NORM_BOUND_MARGIN = 1.0 + 2.0 ** -6
SCORE_BOUND_SLACK = 2.0
ALIBI_SLOPE_PIECES = 3
ALIBI_POS_RADIX = 16

V7X_VMEM_BYTES = 64 * 1024 * 1024
V7X_MXU_DIM = 256
V7X_MXUS = 2
LOG2_E = 1.4426950408889634
SUBLANES = 8
COL_REDUCE_WAYS = 8


def _vmem_limit(estimate_bytes):
    return int(min(max(estimate_bytes, 16 * 1024 * 1024), V7X_VMEM_BYTES - 8 * 1024 * 1024))


def _rms(xf, g, eps):
    ms = jnp.mean(xf * xf, axis=-1, keepdims=True)
    return xf * lax.rsqrt(ms + eps) * g


def _dot(a, b):
    return jnp.dot(a, b, preferred_element_type=F32)


def _dot_nt(a, b):
    return lax.dot_general(a, b, (((1,), (1,)), ((), ())), preferred_element_type=F32)


def _const_spec(shape):
    nd = len(shape)
    return pl.BlockSpec(shape, lambda *_: (0,) * nd, pipeline_mode=pl.Buffered(1))


def _mem_kv_kernel(mem_ref, g_ref, w_ref, k_ref, v_ref, *, mem_width):
    mem_n = _rms(mem_ref[...], g_ref[...], NORM_EPS).astype(BF16)
    kv = _dot(mem_n, w_ref[0])
    head_dim = mem_width // MEM_HEADS
    k_ref[0] = (kv[:, :mem_width] * head_dim ** -0.5).astype(BF16)
    v_ref[0] = kv[:, mem_width:].astype(BF16)


def _mem_kv(mem2d, g_mem, w_kv):
    depth, d_model, two_w = w_kv.shape
    mem_width = two_w // 2
    rows = mem2d.shape[0]
    return pl.pallas_call(
        functools.partial(_mem_kv_kernel, mem_width=mem_width),
        grid=(depth,),
        in_specs=[
            pl.BlockSpec((rows, d_model), lambda i: (0, 0)),
            pl.BlockSpec((1, d_model), lambda i: (0, 0)),
            pl.BlockSpec((1, d_model, two_w), lambda i: (i, 0, 0)),
        ],
        out_specs=[
            pl.BlockSpec((1, rows, mem_width), lambda i: (i, 0, 0)),
            pl.BlockSpec((1, rows, mem_width), lambda i: (i, 0, 0)),
        ],
        out_shape=[jax.ShapeDtypeStruct((depth, rows, mem_width), BF16)] * 2,
        name="mem_kv",
    )(mem2d, g_mem.reshape(1, d_model), w_kv)


def _in_proj_kernel(x_ref, g_ref, w_ref, q_ref, k_ref, vt_ref, mq_ref, *, q_scale):
    d = x_ref.shape[1]
    hn = _rms(x_ref[...], g_ref[...], NORM_EPS).astype(BF16)
    q_ref[...] = (_dot(hn, w_ref[:, 0:d]) * q_scale).astype(BF16)
    k_ref[...] = _dot(hn, w_ref[:, d:2 * d]).astype(BF16)
    vt_ref[0] = _dot(hn, w_ref[:, 2 * d:3 * d]).T.astype(BF16)
    mq_ref[...] = _dot(hn, w_ref[:, 3 * d:]).astype(BF16)


def _in_proj(x2d, g, w, *, q_scale, tm):
    t, d = x2d.shape
    n = w.shape[1]
    mw = n - 3 * d
    est = 2 * tm * d * 4 + d * n * 2 + 2 * tm * n * 2 + 3 * tm * d * 4
    return pl.pallas_call(
        functools.partial(_in_proj_kernel, q_scale=q_scale),
        grid=(t // tm,),
        in_specs=[
            pl.BlockSpec((tm, d), lambda i: (i, 0)),
            _const_spec((1, d)),
            _const_spec((d, n)),
        ],
        out_specs=[
            pl.BlockSpec((tm, d), lambda i: (i, 0)),
            pl.BlockSpec((tm, d), lambda i: (i, 0)),
            pl.BlockSpec((1, d, tm), lambda i: (i, 0, 0)),
            pl.BlockSpec((tm, mw), lambda i: (i, 0)),
        ],
        out_shape=[jax.ShapeDtypeStruct((t, d), BF16)] * 2
        + [jax.ShapeDtypeStruct((t // tm, d, tm), BF16), jax.ShapeDtypeStruct((t, mw), BF16)],
        compiler_params=pltpu.CompilerParams(
            dimension_semantics=("arbitrary",), vmem_limit_bytes=_vmem_limit(est)),
        name="in_proj",
    )(x2d, g.reshape(1, d), w)


def _conv_proj_kernel(x_ref, g_ref, w_ref, cw_ref, mix_ref, mq_ref, u_ref, *, tiles_per_seq):
    tm, d = x_ref.shape
    i = pl.program_id(0)

    @pl.when(i % tiles_per_seq == 0)
    def _():
        u_ref[0:SUBLANES, :] = jnp.zeros((SUBLANES, d), F32)

    hn = _rms(x_ref[...], g_ref[...], NORM_EPS).astype(BF16)
    gate_c = _dot(hn, w_ref[:, d:2 * d])
    h = _dot(hn, w_ref[:, 2 * d:3 * d])
    u_ref[SUBLANES:SUBLANES + tm, :] = gate_c * h
    y = cw_ref[CONV_WIDTH - 1:CONV_WIDTH, :] * u_ref[SUBLANES:SUBLANES + tm, :]
    for tap in range(CONV_WIDTH - 1):
        shift = CONV_WIDTH - 1 - tap
        y = y + cw_ref[tap:tap + 1, :] * u_ref[SUBLANES - shift:SUBLANES - shift + tm, :]
    gate_b = _dot(hn, w_ref[:, 0:d])
    mix_ref[...] = (gate_b * y).astype(BF16)
    mq_ref[...] = _dot(hn, w_ref[:, 3 * d:]).astype(BF16)
    u_ref[0:SUBLANES, :] = u_ref[tm:tm + SUBLANES, :]


def _conv_proj(x2d, g, w, conv_w, *, seq, tm):
    t, d = x2d.shape
    n = w.shape[1]
    mw = n - 3 * d
    est = 2 * tm * d * 4 + d * n * 2 + 2 * tm * (d + mw) * 2 + 5 * tm * d * 4
    return pl.pallas_call(
        functools.partial(_conv_proj_kernel, tiles_per_seq=seq // tm),
        grid=(t // tm,),
        in_specs=[
            pl.BlockSpec((tm, d), lambda i: (i, 0)),
            _const_spec((1, d)),
            _const_spec((d, n)),
            _const_spec((CONV_WIDTH, d)),
        ],
        out_specs=[
            pl.BlockSpec((tm, d), lambda i: (i, 0)),
            pl.BlockSpec((tm, mw), lambda i: (i, 0)),
        ],
        out_shape=[jax.ShapeDtypeStruct((t, d), BF16), jax.ShapeDtypeStruct((t, mw), BF16)],
        scratch_shapes=[pltpu.VMEM((tm + SUBLANES, d), F32)],
        compiler_params=pltpu.CompilerParams(
            dimension_semantics=("arbitrary",), vmem_limit_bytes=_vmem_limit(est)),
        name="conv_proj",
    )(x2d, g.reshape(1, d), w, conv_w)


def _col_reduce(x, op, reduce_fn):
    rows = x.shape[0]
    parts = [x[r * (rows // COL_REDUCE_WAYS):(r + 1) * (rows // COL_REDUCE_WAYS), :]
             for r in range(COL_REDUCE_WAYS)]
    while len(parts) > 1:
        parts = [op(parts[2 * r], parts[2 * r + 1]) for r in range(len(parts) // 2)]
    return reduce_fn(parts[0], axis=0, keepdims=True)


def _tile_validity(key_off, n_keys, qry_off, n_qry, strict):
    last_ok = key_off + n_keys - 1 < qry_off if strict else key_off + n_keys - 1 <= qry_off
    first_bad = key_off >= qry_off + n_qry - 1 if strict else key_off > qry_off + n_qry - 1
    return "all" if last_ok else ("none" if first_bad else "some")


def _sb_kernel(q_ref, k_ref, vt_ref, tri_ref, o_ref, run_ref, acc_ref, z_ref, *, tq, tk, qc, sub):
    i = pl.program_id(2)
    run_ref[...] = jnp.zeros_like(run_ref)
    acc_ref[...] = jnp.zeros_like(acc_ref)

    def load_kv(jb):
        return k_ref[0, pl.ds(pl.multiple_of(jb * tk, tk), tk), :], vt_ref[0, jb]

    def scores(c, ks):
        return jnp.minimum(_dot_nt(ks, q_ref[0, c * qc:(c + 1) * qc, :]), EXP2_ARG_MAX)

    def log_weights(c, z2, mask):
        cols = slice(c * qc, (c + 1) * qc)
        u = jnp.log2(1.0 + jnp.exp2(z2))
        if mask is not None:
            u = jnp.where(mask, u, 0.0)
        u_bf = u.astype(BF16)
        run = run_ref[:, cols]
        log_a = [None] * (tk // sub)
        for sb in reversed(range(tk // sub)):
            rows = slice(sb * sub, (sb + 1) * sub)
            suffix = _dot(tri_ref[...], u_bf[rows, :])
            log_a[sb] = z2[rows, :] - suffix - run
            run = run + suffix[0:1, :]
        run_ref[:, cols] = run
        return jnp.concatenate(log_a, axis=0)

    def accumulate(c, log_a, vt, mask):
        a = jnp.exp2(log_a)
        if mask is not None:
            a = jnp.where(mask, a, 0.0)
        acc_ref[:, c * qc:(c + 1) * qc] += _dot(vt, a.astype(BF16))

    assert (tq // tk) % 2 == 0
    n_before = i * (tq // tk)
    chunks = range(tq // qc)

    def prefetch_scores(jb, slot, active=chunks):
        ks, _ = load_kv(jnp.maximum(jb, 0))
        for c in active:
            z_ref[slot, c] = scores(c, ks)

    def process(jb, slot, active):
        _, vt = load_kv(jb)
        k_next, _ = load_kv(jnp.maximum(jb - 1, 0))
        for c in active:
            log_a = log_weights(c, z_ref[slot, c], None)
            z_ref[1 - slot, c] = scores(c, k_next)
            accumulate(c, log_a, vt, None)

    key = lax.broadcasted_iota(jnp.int32, (tk, qc), 0)
    qry = lax.broadcasted_iota(jnp.int32, (tk, qc), 1)
    diag = []
    for kb in reversed(range(tq // tk)):
        for c in chunks:
            validity = _tile_validity(kb * tk, tk, c * qc, qc, strict=True)
            if validity != "none":
                diag.append((c, kb, None if validity == "all" else key + kb * tk < qry + c * qc))
    kv = {kb: load_kv(i * (tq // tk) + kb) for kb in range(tq // tk)}
    first = min(2, len(diag))
    z2 = [scores(c, kv[kb][0]) for c, kb, _ in diag[:first]]
    log_a = [log_weights(diag[0][0], z2[0], diag[0][2])]
    z2 += [scores(c, kv[kb][0]) for c, kb, _ in diag[first:]]
    prefetch_scores(n_before - 1, 0)
    log_a += [log_weights(c, z2[n], mask) for n, (c, _, mask) in enumerate(diag) if n > 0]
    for n, (c, kb, mask) in enumerate(diag):
        accumulate(c, log_a[n], kv[kb][1], mask)

    def weights_remain(min_run):
        return min_run <= EXP2_ARG_MAX - F32_EXP2_UNDERFLOW_ARG

    def visit_key_blocks(active):
        def min_run():
            return jnp.min(run_ref[:, active[0] * qc:(active[-1] + 1) * qc])

        def keep_going(carry):
            jj, least = carry
            return jnp.logical_and(jj < n_before // 2, weights_remain(least))

        def body(carry):
            jj, _ = carry
            jb = n_before - 1 - 2 * jj
            process(jb, 0, active)

            @pl.when(weights_remain(min_run()))
            def _():
                process(jb - 1, 1, active)

            return jj + 1, min_run()

        lax.while_loop(keep_going, body, (jnp.int32(0), min_run()))

    if len(chunks) > 1:
        later_remain = weights_remain(jnp.min(run_ref[:, qc:]))

        @pl.when(later_remain)
        def _():
            visit_key_blocks(tuple(chunks))

        @pl.when(jnp.logical_not(later_remain))
        def _():
            visit_key_blocks((0,))
    else:
        visit_key_blocks(tuple(chunks))

    o_ref[0] = acc_ref[...].T.astype(o_ref.dtype)


def _sb_attention(q, k, vt, *, tq):
    b, s, d = q.shape
    tk = vt.shape[3]
    heads = d // HEAD_DIM
    qc = min(tq, V7X_MXUS * V7X_MXU_DIM)
    sub = min(tk, V7X_MXU_DIM)
    r = jnp.arange(sub)
    tri = (r[None, :] >= r[:, None]).astype(BF16)
    est = 2 * 2 * s * HEAD_DIM * 2 + 12 * (tq // qc) * tk * qc * 4 + 4 * HEAD_DIM * tq * 4
    return pl.pallas_call(
        functools.partial(_sb_kernel, tq=tq, tk=tk, qc=qc, sub=sub),
        grid=(b, heads, s // tq),
        in_specs=[
            pl.BlockSpec((1, tq, HEAD_DIM), lambda bi, h, i: (bi, i, h)),
            pl.BlockSpec((1, s, HEAD_DIM), lambda bi, h, i: (bi, 0, h)),
            pl.BlockSpec((1, s // tk, HEAD_DIM, tk), lambda bi, h, i: (bi, 0, h, 0)),
            _const_spec((sub, sub)),
        ],
        out_specs=pl.BlockSpec((1, tq, HEAD_DIM), lambda bi, h, i: (bi, i, h)),
        out_shape=jax.ShapeDtypeStruct((b, s, d), BF16),
        scratch_shapes=[pltpu.VMEM((1, tq), F32), pltpu.VMEM((HEAD_DIM, tq), F32),
                        pltpu.VMEM((2, tq // qc, tk, qc), F32)],
        compiler_params=pltpu.CompilerParams(
            dimension_semantics=("arbitrary", "arbitrary", "arbitrary"),
            vmem_limit_bytes=_vmem_limit(est)),
        name="sb_attention",
    )(q, k, vt, tri)


def _diff_kernel(slope2_ref, lq1_ref, lk1_ref, lq2_ref, lk2_ref, g_ref, qf_ref, kf_ref,
                 q1_ref, q2_ref, k1_ref, k2_ref, vt_ref, o_ref, m_ref, l_ref, acc_ref, s_ref,
                 kmax2_ref,
                 *, tq, tk, qc, lambda_init):
    h = pl.program_id(1)
    i = pl.program_id(2)
    slope2 = slope2_ref[h]
    m_ref[...] = jnp.full_like(m_ref, MASK_VALUE)
    l_ref[...] = jnp.zeros_like(l_ref)
    acc_ref[...] = jnp.zeros_like(acc_ref)
    qf = jnp.broadcast_to(qf_ref[0], (qc, qf_ref.shape[2]))
    q_refs = (q1_ref, q2_ref)
    k_refs = (k1_ref, k2_ref)

    def load_kv(jb):
        start = pl.multiple_of(jb * tk, tk)
        k_aug = [jnp.concatenate([k_refs[mp][0, pl.ds(start, tk), :], kf_ref[...]], axis=1)
                 for mp in range(2)]
        return k_aug, vt_ref[0, jb]

    def scores(mp, c, k_aug):
        q_aug = jnp.concatenate([q_refs[mp][0, c * qc:(c + 1) * qc, :], qf], axis=1)
        return _dot_nt(k_aug, q_aug)

    def probabilities(mp, c, s, off, mask):
        cols = slice(c * qc, (c + 1) * qc)
        if mask is not None:
            s = jnp.where(mask, s, MASK_VALUE)
        m = m_ref[mp, :, cols]
        m_new = jnp.maximum(m, _col_reduce(s, jnp.maximum, jnp.max) + off)
        alpha = jnp.exp2(m - m_new)
        p = jnp.exp2(s - (m_new - off))
        m_ref[mp, :, cols] = m_new
        l_ref[mp, :, cols] = alpha * l_ref[mp, :, cols] + _col_reduce(p, jnp.add, jnp.sum)
        return alpha, p.astype(BF16)

    def accumulate(mp, c, alpha, p_bf, vt):
        cols = slice(c * qc, (c + 1) * qc)
        acc_ref[mp, :, cols] = alpha * acc_ref[mp, :, cols] + _dot(vt, p_bf)

    assert (tq // tk) % 2 == 0
    n_before = i * (tq // tk)
    tiles = [(mp, c) for c in range(tq // qc) for mp in range(2)]

    def prefetch_scores(jb, slot):
        k_aug, _ = load_kv(jnp.maximum(jb, 0))
        for n, t in enumerate(tiles):
            s_ref[slot, n] = scores(*t, k_aug[t[0]])

    def process(jb, slot):
        _, vt = load_kv(jb)
        off = slope2 * (jb * tk - i * tq).astype(F32)
        k_next, _ = load_kv(jnp.maximum(jb - 1, 0))
        for n, t in enumerate(tiles):
            accumulate(*t, *probabilities(*t, s_ref[slot, n], off, None), vt)
            s_ref[1 - slot, n] = scores(*t, k_next[t[0]])

    key = lax.broadcasted_iota(jnp.int32, (tk, qc), 0)
    qry = lax.broadcasted_iota(jnp.int32, (tk, qc), 1)
    diag = []
    for kb in reversed(range(tq // tk)):
        for mp, c in tiles:
            validity = _tile_validity(kb * tk, tk, c * qc, qc, strict=False)
            if validity != "none":
                diag.append((mp, c, kb, None if validity == "all" else key + kb * tk <= qry + c * qc))
    kv = {kb: load_kv(i * (tq // tk) + kb) for kb in range(tq // tk)}
    s_diag = [scores(mp, c, kv[kb][0][mp]) for mp, c, kb, _ in diag]
    prefetch_scores(n_before - 1, 0)
    ap = [probabilities(mp, c, s_diag[n], slope2 * (kb * tk), mask)
          for n, (mp, c, kb, mask) in enumerate(diag)]
    for n, (mp, c, kb, _) in enumerate(diag):
        accumulate(mp, c, *ap[n], kv[kb][1])

    @pl.when(i == 0)
    def _():
        for mp in range(2):
            def norm_step(j, best):
                ks = k_refs[mp][0, pl.ds(pl.multiple_of(j * tk, tk), tk), :].astype(F32)
                best = jnp.maximum(best, jnp.max(jnp.sum(ks * ks, axis=1, keepdims=True)))
                kmax2_ref[mp, j] = best
                return best
            lax.fori_loop(0, k1_ref.shape[1] // tk, norm_step, jnp.float32(0.0))

    ones = jnp.ones((SUBLANES, q1_ref.shape[2]), BF16)
    qnorm2 = []
    for mp in range(2):
        q32 = q_refs[mp][0].astype(F32)
        qnorm2.append(_dot_nt(ones, (q32 * q32).astype(BF16))[0:1, :] * NORM_BOUND_MARGIN)

    def score_gap(jb):
        jb = jnp.maximum(jb, 0)
        gap = [jnp.max(jnp.sqrt(qnorm2[mp] * kmax2_ref[mp, jb]) - m_ref[mp]) for mp in range(2)]
        return (jnp.maximum(gap[0], gap[1]) + slope2 * (tk - 1)
                + slope2 * (jb * tk - i * tq).astype(F32))

    def keep_going(carry):
        jj, gap = carry
        return jnp.logical_and(jj < n_before // 2, gap >= F32_EXP2_UNDERFLOW_ARG - SCORE_BOUND_SLACK)

    def body(carry):
        jj, _ = carry
        jb = n_before - 1 - 2 * jj
        process(jb, 0)
        process(jb - 1, 1)
        return jj + 1, score_gap(jb - 2)

    lax.while_loop(keep_going, body, (jnp.int32(0), score_gap(n_before - 1)))

    lam = (jnp.exp(jnp.sum(lq1_ref[...] * lk1_ref[...], axis=1, keepdims=True))
           - jnp.exp(jnp.sum(lq2_ref[...] * lk2_ref[...], axis=1, keepdims=True))
           + lambda_init)
    o_t = acc_ref[0] / l_ref[0] - lam * (acc_ref[1] / l_ref[1])
    ms = jnp.mean(o_t * o_t, axis=0, keepdims=True)
    o_t = o_t * lax.rsqrt(ms + HEAD_NORM_EPS) * (g_ref[...] * (1.0 - lambda_init))
    o_ref[0] = o_t.T.astype(o_ref.dtype)


def _alibi_features(slope2, blk):
    pieces = []
    rest = slope2
    for _ in range(ALIBI_SLOPE_PIECES):
        piece = rest.astype(BF16)
        pieces.append(piece)
        rest = rest - piece.astype(F32)
    qf = jnp.stack([p * r for p in pieces for r in (ALIBI_POS_RADIX, 1)], axis=1)
    pos = jnp.arange(blk)
    digits = jnp.stack([pos // ALIBI_POS_RADIX, pos % ALIBI_POS_RADIX] * ALIBI_SLOPE_PIECES, axis=1)
    pad = HEAD_DIM - 2 * ALIBI_SLOPE_PIECES
    qf = jnp.pad(qf.astype(BF16), ((0, 0), (0, pad)))[:, None, :]
    kf = jnp.pad(digits.astype(BF16), ((0, 0), (0, pad)))
    return qf, kf


def _diff_attention(q, k, vt, lq1, lk1, lq2, lk2, g_head, *, layer_idx, tq):
    b, s, d = q.shape
    tk = vt.shape[3]
    heads = d // DIFF_V_DIM
    qc = min(tq, V7X_MXUS * V7X_MXU_DIM)
    assert tk <= ALIBI_POS_RADIX * 256
    lambda_init = 0.8 - 0.6 * math.exp(-0.3 * layer_idx)
    slope2 = LOG2_E * 2.0 ** (-8.0 * jnp.arange(1, heads + 1, dtype=F32) / heads)
    qf, kf = _alibi_features(slope2, tk)
    vec = lambda a: a.reshape(1, -1).astype(F32)
    est = (2 * s * (2 * HEAD_DIM + DIFF_V_DIM) * 2 + 12 * 2 * (tq // qc) * tk * qc * 4
           + 4 * tq * DIFF_V_DIM * 4)
    return pl.pallas_call(
        functools.partial(_diff_kernel, tq=tq, tk=tk, qc=qc, lambda_init=lambda_init),
        grid=(b, heads, s // tq),
        in_specs=[
            pl.BlockSpec(memory_space=pltpu.SMEM),
            _const_spec((1, HEAD_DIM)), _const_spec((1, HEAD_DIM)),
            _const_spec((1, HEAD_DIM)), _const_spec((1, HEAD_DIM)),
            _const_spec((DIFF_V_DIM, 1)),
            pl.BlockSpec((1, 1, HEAD_DIM), lambda bi, h, i: (h, 0, 0)),
            _const_spec((tk, HEAD_DIM)),
            pl.BlockSpec((1, tq, HEAD_DIM), lambda bi, h, i: (bi, i, 2 * h)),
            pl.BlockSpec((1, tq, HEAD_DIM), lambda bi, h, i: (bi, i, 2 * h + 1)),
            pl.BlockSpec((1, s, HEAD_DIM), lambda bi, h, i: (bi, 0, 2 * h)),
            pl.BlockSpec((1, s, HEAD_DIM), lambda bi, h, i: (bi, 0, 2 * h + 1)),
            pl.BlockSpec((1, s // tk, DIFF_V_DIM, tk), lambda bi, h, i: (bi, 0, h, 0)),
        ],
        out_specs=pl.BlockSpec((1, tq, DIFF_V_DIM), lambda bi, h, i: (bi, i, h)),
        out_shape=jax.ShapeDtypeStruct((b, s, d), BF16),
        scratch_shapes=[pltpu.VMEM((2, 1, tq), F32), pltpu.VMEM((2, 1, tq), F32),
                        pltpu.VMEM((2, DIFF_V_DIM, tq), F32),
                        pltpu.VMEM((2, 2 * (tq // qc), tk, qc), F32),
                        pltpu.SMEM((2, s // tk), F32)],
        compiler_params=pltpu.CompilerParams(
            dimension_semantics=("arbitrary", "arbitrary", "arbitrary"),
            vmem_limit_bytes=_vmem_limit(est)),
        name="diff_attention",
    )(slope2, vec(lq1), vec(lk1), vec(lq2), vec(lk2), g_head.reshape(-1, 1).astype(F32), qf, kf,
      q, q, k, k, vt)


def _post_kernel(mix_ref, mq_ref, x_ref, km_ref, vm_ref, wo_ref, g_ref, w1_ref, w2_ref, gf_ref,
                 o_ref, *, ff_chunks, final_norm):
    tm, d = x_ref.shape
    mw = mq_ref.shape[1]
    d_ff = w2_ref.shape[0]
    head_dim = mw // MEM_HEADS

    mq = mq_ref[...]
    km = km_ref[0]
    vm = vm_ref[0]
    lane_head = lax.broadcasted_iota(jnp.int32, (tm, mw), 1) // head_dim
    mo = jnp.zeros((tm, mw), F32)
    for hd in range(MEM_HEADS):
        in_head = lane_head == hd
        s = _dot_nt(jnp.where(in_head, mq, jnp.zeros_like(mq)), km)
        p = jnp.exp(s - jnp.max(s, axis=1, keepdims=True))
        l = jnp.sum(p, axis=1, keepdims=True)
        mo = jnp.where(in_head, _dot(p.astype(BF16), vm) / l, mo)

    y = x_ref[...] + _dot(mix_ref[...], wo_ref[0:d, :]) + _dot(mo.astype(BF16), wo_ref[d:, :])

    yn = _rms(y, g_ref[...], NORM_EPS).astype(BF16)
    hidden = []
    for lo, hi in ff_chunks:
        gate = _dot(yn, w1_ref[:, lo:hi])
        up = _dot(yn, w1_ref[:, d_ff + lo:d_ff + hi])
        hidden.append((gate * (1.0 / (1.0 + jnp.exp(-gate))) * up).astype(BF16))
    y = y + _dot(jnp.concatenate(hidden, axis=1), w2_ref[...])
    if final_norm:
        y = _rms(y, gf_ref[...], NORM_EPS)
    o_ref[...] = y


def _ff_chunks(d_ff, n_chunks, align):
    tiles = d_ff // align
    bounds = [align * ((tiles * c) // n_chunks) for c in range(n_chunks)] + [d_ff]
    return tuple((bounds[c], bounds[c + 1]) for c in range(n_chunks))


def _post(mix2d, mq2d, x2d, km, vm, wo, g_ffn, w1, w2, g_final, *, seq, tm, final_norm):
    t, d = x2d.shape
    mw = mq2d.shape[1]
    mem_len = km.shape[1]
    d_ff = w2.shape[0]
    tiles_per_seq = seq // tm
    ff_chunks = _ff_chunks(d_ff, 2, 256)
    chunk = max(hi - lo for lo, hi in ff_chunks)
    est = ((wo.size + w1.size + w2.size) * 2 + 2 * tm * (d + mw) * 2 + 4 * tm * d * 4
           + tm * chunk * 12 + 4 * tm * d * 4)
    return pl.pallas_call(
        functools.partial(_post_kernel, ff_chunks=ff_chunks, final_norm=final_norm),
        grid=(t // tm,),
        in_specs=[
            pl.BlockSpec((tm, d), lambda i: (i, 0)),
            pl.BlockSpec((tm, mw), lambda i: (i, 0)),
            pl.BlockSpec((tm, d), lambda i: (i, 0)),
            pl.BlockSpec((1, mem_len, mw), lambda i: (i // tiles_per_seq, 0, 0)),
            pl.BlockSpec((1, mem_len, mw), lambda i: (i // tiles_per_seq, 0, 0)),
            _const_spec(wo.shape),
            _const_spec((1, d)),
            _const_spec(w1.shape),
            _const_spec(w2.shape),
            _const_spec((1, d)),
        ],
        out_specs=pl.BlockSpec((tm, d), lambda i: (i, 0)),
        out_shape=jax.ShapeDtypeStruct((t, d), F32),
        compiler_params=pltpu.CompilerParams(
            dimension_semantics=("arbitrary",), vmem_limit_bytes=_vmem_limit(est)),
        name="post",
    )(mix2d, mq2d, x2d, km, vm, wo, g_ffn.reshape(1, d), w1, w2, g_final.reshape(1, d))


def kernel(x, mem, g_mix, w_in, w_mem_kv, w_o, g_ffn, w_ffn_in, w_ffn_out,
           lam_q1, lam_k1, lam_q2, lam_k2, g_diff_head, conv_w, g_mem, g_final):
    b, seq, d = x.shape
    mem_len = mem.shape[1]
    depth = w_in.shape[0]
    t = b * seq
    tm = min(512, seq)
    sb_tq = min(1024, seq)
    diff_tq = min(1024, seq)

    km_all, vm_all = _mem_kv(mem.reshape(b * mem_len, d), g_mem, w_mem_kv.astype(BF16))
    mw = km_all.shape[-1]
    km_all = km_all.reshape(depth, b, mem_len, mw)
    vm_all = vm_all.reshape(depth, b, mem_len, mw)

    x2d = x.reshape(t, d)
    for i in range(depth):
        kind = i % N_MIXERS
        j = i // N_MIXERS
        w = w_in[i].astype(BF16)
        if kind == 2:
            mix, mq = _conv_proj(x2d, g_mix[i], w, conv_w[j], seq=seq, tm=tm)
        else:
            q, k, vt, mq = _in_proj(x2d, g_mix[i], w, q_scale=LOG2_E * HEAD_DIM ** -0.5, tm=tm)
            q, k = (a.reshape(b, seq, d) for a in (q, k))
            vt = vt.reshape(b, seq // tm, d, tm)
            if kind == 0:
                mix = _sb_attention(q, k, vt, tq=sb_tq)
            else:
                mix = _diff_attention(q, k, vt, lam_q1[j], lam_k1[j], lam_q2[j], lam_k2[j],
                                      g_diff_head[j], layer_idx=i, tq=diff_tq)
            mix = mix.reshape(t, d)
        x2d = _post(mix, mq, x2d, km_all[i], vm_all[i], w_o[i].astype(BF16), g_ffn[i],
                    w_ffn_in[i].astype(BF16), w_ffn_out[i].astype(BF16), g_final,
                    seq=seq, tm=tm, final_norm=(i == depth - 1))
    return x2d.reshape(b, seq, d)
```

```python
import functools
import math

import jax
import jax.numpy as jnp
from jax import lax
from jax.experimental import pallas as pl
from jax.experimental.pallas import tpu as pltpu

F32 = jnp.float32
BF16 = jnp.bfloat16

N_MIXERS = 3
HEAD_DIM = 128
DIFF_V_DIM = 2 * HEAD_DIM
MEM_HEADS = 4
CONV_WIDTH = 3
NORM_EPS = 1e-6
HEAD_NORM_EPS = 1e-5
MASK_VALUE = -1e30
EXP2_ARG_MAX = 126.0
F32_EXP2_UNDERFLOW_ARG = -150.0
NORM_BOUND_MARGIN = 1.0 + 2.0 ** -6
SCORE_BOUND_SLACK = 2.0
ALIBI_SLOPE_PIECES = 3
ALIBI_POS_RADIX = 16

V7X_VMEM_BYTES = 64 * 1024 * 1024
V7X_MXU_DIM = 256
V7X_MXUS = 2
LOG2_E = 1.4426950408889634
SUBLANES = 8
COL_REDUCE_WAYS = 32


def _vmem_limit(estimate_bytes):
    return int(min(max(estimate_bytes, 16 * 1024 * 1024), V7X_VMEM_BYTES - 8 * 1024 * 1024))


def _rms(xf, g, eps):
    ms = jnp.mean(xf * xf, axis=-1, keepdims=True)
    return xf * lax.rsqrt(ms + eps) * g


def _dot(a, b):
    return jnp.dot(a, b, preferred_element_type=F32)


def _dot_nt(a, b):
    return lax.dot_general(a, b, (((1,), (1,)), ((), ())), preferred_element_type=F32)


def _const_spec(shape):
    nd = len(shape)
    return pl.BlockSpec(shape, lambda *_: (0,) * nd, pipeline_mode=pl.Buffered(1))


def _mem_kv_kernel(mem_ref, g_ref, w_ref, k_ref, v_ref, *, mem_width):
    mem_n = _rms(mem_ref[...], g_ref[...], NORM_EPS).astype(BF16)
    kv = _dot(mem_n, w_ref[0])
    head_dim = mem_width // MEM_HEADS
    k_ref[0] = (kv[:, :mem_width] * head_dim ** -0.5).astype(BF16)
    v_ref[0] = kv[:, mem_width:].astype(BF16)


def _mem_kv(mem2d, g_mem, w_kv):
    depth, d_model, two_w = w_kv.shape
    mem_width = two_w // 2
    rows = mem2d.shape[0]
    return pl.pallas_call(
        functools.partial(_mem_kv_kernel, mem_width=mem_width),
        grid=(depth,),
        in_specs=[
            pl.BlockSpec((rows, d_model), lambda i: (0, 0)),
            pl.BlockSpec((1, d_model), lambda i: (0, 0)),
            pl.BlockSpec((1, d_model, two_w), lambda i: (i, 0, 0)),
        ],
        out_specs=[
            pl.BlockSpec((1, rows, mem_width), lambda i: (i, 0, 0)),
            pl.BlockSpec((1, rows, mem_width), lambda i: (i, 0, 0)),
        ],
        out_shape=[jax.ShapeDtypeStruct((depth, rows, mem_width), BF16)] * 2,
        name="mem_kv",
    )(mem2d, g_mem.reshape(1, d_model), w_kv)


def _in_proj_kernel(x_ref, g_ref, w_ref, q_ref, k_ref, vt_ref, mq_ref, *, q_scale):
    d = x_ref.shape[1]
    hn = _rms(x_ref[...], g_ref[...], NORM_EPS).astype(BF16)
    q_ref[...] = (_dot(hn, w_ref[:, 0:d]) * q_scale).astype(BF16)
    k_ref[...] = _dot(hn, w_ref[:, d:2 * d]).astype(BF16)
    vt_ref[0] = _dot(hn, w_ref[:, 2 * d:3 * d]).T.astype(BF16)
    mq_ref[...] = _dot(hn, w_ref[:, 3 * d:]).astype(BF16)


def _in_proj(x2d, g, w, *, q_scale, tm):
    t, d = x2d.shape
    n = w.shape[1]
    mw = n - 3 * d
    est = 2 * tm * d * 4 + d * n * 2 + 2 * tm * n * 2 + 3 * tm * d * 4
    return pl.pallas_call(
        functools.partial(_in_proj_kernel, q_scale=q_scale),
        grid=(t // tm,),
        in_specs=[
            pl.BlockSpec((tm, d), lambda i: (i, 0)),
            _const_spec((1, d)),
            _const_spec((d, n)),
        ],
        out_specs=[
            pl.BlockSpec((tm, d), lambda i: (i, 0)),
            pl.BlockSpec((tm, d), lambda i: (i, 0)),
            pl.BlockSpec((1, d, tm), lambda i: (i, 0, 0)),
            pl.BlockSpec((tm, mw), lambda i: (i, 0)),
        ],
        out_shape=[jax.ShapeDtypeStruct((t, d), BF16)] * 2
        + [jax.ShapeDtypeStruct((t // tm, d, tm), BF16), jax.ShapeDtypeStruct((t, mw), BF16)],
        compiler_params=pltpu.CompilerParams(
            dimension_semantics=("arbitrary",), vmem_limit_bytes=_vmem_limit(est)),
        name="in_proj",
    )(x2d, g.reshape(1, d), w)


def _conv_proj_kernel(x_ref, g_ref, w_ref, cw_ref, mix_ref, mq_ref, u_ref, *, tiles_per_seq):
    tm, d = x_ref.shape
    i = pl.program_id(0)

    @pl.when(i % tiles_per_seq == 0)
    def _():
        u_ref[0:SUBLANES, :] = jnp.zeros((SUBLANES, d), F32)

    hn = _rms(x_ref[...], g_ref[...], NORM_EPS).astype(BF16)
    gate_c = _dot(hn, w_ref[:, d:2 * d])
    h = _dot(hn, w_ref[:, 2 * d:3 * d])
    u_ref[SUBLANES:SUBLANES + tm, :] = gate_c * h
    y = cw_ref[CONV_WIDTH - 1:CONV_WIDTH, :] * u_ref[SUBLANES:SUBLANES + tm, :]
    for tap in range(CONV_WIDTH - 1):
        shift = CONV_WIDTH - 1 - tap
        y = y + cw_ref[tap:tap + 1, :] * u_ref[SUBLANES - shift:SUBLANES - shift + tm, :]
    gate_b = _dot(hn, w_ref[:, 0:d])
    mix_ref[...] = (gate_b * y).astype(BF16)
    mq_ref[...] = _dot(hn, w_ref[:, 3 * d:]).astype(BF16)
    u_ref[0:SUBLANES, :] = u_ref[tm:tm + SUBLANES, :]


def _conv_proj(x2d, g, w, conv_w, *, seq, tm):
    t, d = x2d.shape
    n = w.shape[1]
    mw = n - 3 * d
    est = 2 * tm * d * 4 + d * n * 2 + 2 * tm * (d + mw) * 2 + 5 * tm * d * 4
    return pl.pallas_call(
        functools.partial(_conv_proj_kernel, tiles_per_seq=seq // tm),
        grid=(t // tm,),
        in_specs=[
            pl.BlockSpec((tm, d), lambda i: (i, 0)),
            _const_spec((1, d)),
            _const_spec((d, n)),
            _const_spec((CONV_WIDTH, d)),
        ],
        out_specs=[
            pl.BlockSpec((tm, d), lambda i: (i, 0)),
            pl.BlockSpec((tm, mw), lambda i: (i, 0)),
        ],
        out_shape=[jax.ShapeDtypeStruct((t, d), BF16), jax.ShapeDtypeStruct((t, mw), BF16)],
        scratch_shapes=[pltpu.VMEM((tm + SUBLANES, d), F32)],
        compiler_params=pltpu.CompilerParams(
            dimension_semantics=("arbitrary",), vmem_limit_bytes=_vmem_limit(est)),
        name="conv_proj",
    )(x2d, g.reshape(1, d), w, conv_w)


def _col_reduce(x, op, reduce_fn):
    rows = x.shape[0]
    parts = [x[r * (rows // COL_REDUCE_WAYS):(r + 1) * (rows // COL_REDUCE_WAYS), :]
             for r in range(COL_REDUCE_WAYS)]
    while len(parts) > 1:
        parts = [op(parts[2 * r], parts[2 * r + 1]) for r in range(len(parts) // 2)]
    return reduce_fn(parts[0], axis=0, keepdims=True)


def _tile_validity(key_off, n_keys, qry_off, n_qry, strict):
    last_ok = key_off + n_keys - 1 < qry_off if strict else key_off + n_keys - 1 <= qry_off
    first_bad = key_off >= qry_off + n_qry - 1 if strict else key_off > qry_off + n_qry - 1
    return "all" if last_ok else ("none" if first_bad else "some")


def _sb_kernel(q_ref, k_ref, vt_ref, tri_ref, o_ref, run_ref, acc_ref, z_ref, *, tq, tk, qc, sub):
    i = pl.program_id(2)
    run_ref[...] = jnp.zeros_like(run_ref)
    acc_ref[...] = jnp.zeros_like(acc_ref)

    def load_kv(jb):
        return k_ref[0, pl.ds(pl.multiple_of(jb * tk, tk), tk), :], vt_ref[0, jb]

    def scores(c, ks):
        return jnp.minimum(_dot_nt(ks, q_ref[0, c * qc:(c + 1) * qc, :]), EXP2_ARG_MAX)

    def log_weights(c, z2, mask):
        cols = slice(c * qc, (c + 1) * qc)
        u = jnp.log2(1.0 + jnp.exp2(z2))
        if mask is not None:
            u = jnp.where(mask, u, 0.0)
        u_bf = u.astype(BF16)
        run = run_ref[:, cols]
        log_a = [None] * (tk // sub)
        for sb in reversed(range(tk // sub)):
            rows = slice(sb * sub, (sb + 1) * sub)
            suffix = _dot(tri_ref[...], u_bf[rows, :])
            log_a[sb] = z2[rows, :] - suffix - run
            run = run + suffix[0:1, :]
        run_ref[:, cols] = run
        return jnp.concatenate(log_a, axis=0)

    def accumulate(c, log_a, vt, mask):
        a = jnp.exp2(log_a)
        if mask is not None:
            a = jnp.where(mask, a, 0.0)
        acc_ref[:, c * qc:(c + 1) * qc] += _dot(vt, a.astype(BF16))

    assert (tq // tk) % 2 == 0
    n_before = i * (tq // tk)
    chunks = range(tq // qc)

    def prefetch_scores(jb, slot, active=chunks):
        ks, _ = load_kv(jnp.maximum(jb, 0))
        for c in active:
            z_ref[slot, c] = scores(c, ks)

    def process(jb, slot, active):
        _, vt = load_kv(jb)
        k_next, _ = load_kv(jnp.maximum(jb - 1, 0))
        for c in active:
            log_a = log_weights(c, z_ref[slot, c], None)
            z_ref[1 - slot, c] = scores(c, k_next)
            accumulate(c, log_a, vt, None)

    key = lax.broadcasted_iota(jnp.int32, (tk, qc), 0)
    qry = lax.broadcasted_iota(jnp.int32, (tk, qc), 1)
    diag = []
    for kb in reversed(range(tq // tk)):
        for c in chunks:
            validity = _tile_validity(kb * tk, tk, c * qc, qc, strict=True)
            if validity != "none":
                diag.append((c, kb, None if validity == "all" else key + kb * tk < qry + c * qc))
    kv = {kb: load_kv(i * (tq // tk) + kb) for kb in range(tq // tk)}
    first = min(2, len(diag))
    z2 = [scores(c, kv[kb][0]) for c, kb, _ in diag[:first]]
    log_a = [log_weights(diag[0][0], z2[0], diag[0][2])]
    z2 += [scores(c, kv[kb][0]) for c, kb, _ in diag[first:]]
    prefetch_scores(n_before - 1, 0)
    log_a += [log_weights(c, z2[n], mask) for n, (c, _, mask) in enumerate(diag) if n > 0]
    for n, (c, kb, mask) in enumerate(diag):
        accumulate(c, log_a[n], kv[kb][1], mask)

    def weights_remain(min_run):
        return min_run <= EXP2_ARG_MAX - F32_EXP2_UNDERFLOW_ARG

    def visit_key_blocks(active):
        def min_run():
            return jnp.min(run_ref[:, active[0] * qc:(active[-1] + 1) * qc])

        def keep_going(carry):
            jj, least = carry
            return jnp.logical_and(jj < n_before // 2, weights_remain(least))

        def body(carry):
            jj, _ = carry
            jb = n_before - 1 - 2 * jj
            process(jb, 0, active)

            @pl.when(weights_remain(min_run()))
            def _():
                process(jb - 1, 1, active)

            return jj + 1, min_run()

        lax.while_loop(keep_going, body, (jnp.int32(0), min_run()))

    if len(chunks) > 1:
        later_remain = weights_remain(jnp.min(run_ref[:, qc:]))

        @pl.when(later_remain)
        def _():
            visit_key_blocks(tuple(chunks))

        @pl.when(jnp.logical_not(later_remain))
        def _():
            visit_key_blocks((0,))
    else:
        visit_key_blocks(tuple(chunks))

    o_ref[0] = acc_ref[...].T.astype(o_ref.dtype)


def _sb_attention(q, k, vt, *, tq):
    b, s, d = q.shape
    tk = vt.shape[3]
    heads = d // HEAD_DIM
    qc = min(tq, V7X_MXUS * V7X_MXU_DIM)
    sub = min(tk, V7X_MXU_DIM)
    r = jnp.arange(sub)
    tri = (r[None, :] >= r[:, None]).astype(BF16)
    est = 2 * 2 * s * HEAD_DIM * 2 + 12 * (tq // qc) * tk * qc * 4 + 4 * HEAD_DIM * tq * 4
    return pl.pallas_call(
        functools.partial(_sb_kernel, tq=tq, tk=tk, qc=qc, sub=sub),
        grid=(b, heads, s // tq),
        in_specs=[
            pl.BlockSpec((1, tq, HEAD_DIM), lambda bi, h, i: (bi, i, h)),
            pl.BlockSpec((1, s, HEAD_DIM), lambda bi, h, i: (bi, 0, h)),
            pl.BlockSpec((1, s // tk, HEAD_DIM, tk), lambda bi, h, i: (bi, 0, h, 0)),
            _const_spec((sub, sub)),
        ],
        out_specs=pl.BlockSpec((1, tq, HEAD_DIM), lambda bi, h, i: (bi, i, h)),
        out_shape=jax.ShapeDtypeStruct((b, s, d), BF16),
        scratch_shapes=[pltpu.VMEM((1, tq), F32), pltpu.VMEM((HEAD_DIM, tq), F32),
                        pltpu.VMEM((2, tq // qc, tk, qc), F32)],
        compiler_params=pltpu.CompilerParams(
            dimension_semantics=("arbitrary", "arbitrary", "arbitrary"),
            vmem_limit_bytes=_vmem_limit(est)),
        name="sb_attention",
    )(q, k, vt, tri)


def _diff_kernel(slope2_ref, lq1_ref, lk1_ref, lq2_ref, lk2_ref, g_ref, qf_ref, kf_ref,
                 q1_ref, q2_ref, k1_ref, k2_ref, vt_ref, o_ref, m_ref, l_ref, acc_ref, s_ref,
                 kmax2_ref,
                 *, tq, tk, qc, lambda_init):
    h = pl.program_id(1)
    i = pl.program_id(2)
    slope2 = slope2_ref[h]
    m_ref[...] = jnp.full_like(m_ref, MASK_VALUE)
    l_ref[...] = jnp.zeros_like(l_ref)
    acc_ref[...] = jnp.zeros_like(acc_ref)
    qf = jnp.broadcast_to(qf_ref[0], (qc, qf_ref.shape[2]))
    q_refs = (q1_ref, q2_ref)
    k_refs = (k1_ref, k2_ref)

    def load_kv(jb):
        start = pl.multiple_of(jb * tk, tk)
        k_aug = [jnp.concatenate([k_refs[mp][0, pl.ds(start, tk), :], kf_ref[...]], axis=1)
                 for mp in range(2)]
        return k_aug, vt_ref[0, jb]

    def scores(mp, c, k_aug):
        q_aug = jnp.concatenate([q_refs[mp][0, c * qc:(c + 1) * qc, :], qf], axis=1)
        return _dot_nt(k_aug, q_aug)

    def probabilities(mp, c, s, off, mask):
        cols = slice(c * qc, (c + 1) * qc)
        if mask is not None:
            s = jnp.where(mask, s, MASK_VALUE)
        m = m_ref[mp, :, cols]
        m_new = jnp.maximum(m, _col_reduce(s, jnp.maximum, jnp.max) + off)
        alpha = jnp.exp2(m - m_new)
        p = jnp.exp2(s - (m_new - off))
        m_ref[mp, :, cols] = m_new
        l_ref[mp, :, cols] = alpha * l_ref[mp, :, cols] + _col_reduce(p, jnp.add, jnp.sum)
        return alpha, p.astype(BF16)

    def accumulate(mp, c, alpha, p_bf, vt):
        cols = slice(c * qc, (c + 1) * qc)
        acc_ref[mp, :, cols] = alpha * acc_ref[mp, :, cols] + _dot(vt, p_bf)

    assert (tq // tk) % 2 == 0
    n_before = i * (tq // tk)
    tiles = [(mp, c) for c in range(tq // qc) for mp in range(2)]

    def prefetch_scores(jb, slot):
        k_aug, _ = load_kv(jnp.maximum(jb, 0))
        for n, t in enumerate(tiles):
            s_ref[slot, n] = scores(*t, k_aug[t[0]])

    def process(jb, slot):
        _, vt = load_kv(jb)
        off = slope2 * (jb * tk - i * tq).astype(F32)
        k_next, _ = load_kv(jnp.maximum(jb - 1, 0))
        for n, t in enumerate(tiles):
            accumulate(*t, *probabilities(*t, s_ref[slot, n], off, None), vt)
            s_ref[1 - slot, n] = scores(*t, k_next[t[0]])

    key = lax.broadcasted_iota(jnp.int32, (tk, qc), 0)
    qry = lax.broadcasted_iota(jnp.int32, (tk, qc), 1)
    diag = []
    for kb in reversed(range(tq // tk)):
        for mp, c in tiles:
            validity = _tile_validity(kb * tk, tk, c * qc, qc, strict=False)
            if validity != "none":
                diag.append((mp, c, kb, None if validity == "all" else key + kb * tk <= qry + c * qc))
    kv = {kb: load_kv(i * (tq // tk) + kb) for kb in range(tq // tk)}
    s_diag = [scores(mp, c, kv[kb][0][mp]) for mp, c, kb, _ in diag]
    prefetch_scores(n_before - 1, 0)
    ap = [probabilities(mp, c, s_diag[n], slope2 * (kb * tk), mask)
          for n, (mp, c, kb, mask) in enumerate(diag)]
    for n, (mp, c, kb, _) in enumerate(diag):
        accumulate(mp, c, *ap[n], kv[kb][1])

    @pl.when(i == 0)
    def _():
        for mp in range(2):
            def norm_step(j, best):
                ks = k_refs[mp][0, pl.ds(pl.multiple_of(j * tk, tk), tk), :].astype(F32)
                best = jnp.maximum(best, jnp.max(jnp.sum(ks * ks, axis=1, keepdims=True)))
                kmax2_ref[mp, j] = best
                return best
            lax.fori_loop(0, k1_ref.shape[1] // tk, norm_step, jnp.float32(0.0))

    ones = jnp.ones((SUBLANES, q1_ref.shape[2]), BF16)
    qnorm2 = []
    for mp in range(2):
        q32 = q_refs[mp][0].astype(F32)
        qnorm2.append(_dot_nt(ones, (q32 * q32).astype(BF16))[0:1, :] * NORM_BOUND_MARGIN)

    def score_gap(jb):
        jb = jnp.maximum(jb, 0)
        gap = [jnp.max(jnp.sqrt(qnorm2[mp] * kmax2_ref[mp, jb]) - m_ref[mp]) for mp in range(2)]
        return (jnp.maximum(gap[0], gap[1]) + slope2 * (tk - 1)
                + slope2 * (jb * tk - i * tq).astype(F32))

    def keep_going(carry):
        jj, gap = carry
        return jnp.logical_and(jj < n_before // 2, gap >= F32_EXP2_UNDERFLOW_ARG - SCORE_BOUND_SLACK)

    def body(carry):
        jj, _ = carry
        jb = n_before - 1 - 2 * jj
        process(jb, 0)
        process(jb - 1, 1)
        return jj + 1, score_gap(jb - 2)

    lax.while_loop(keep_going, body, (jnp.int32(0), score_gap(n_before - 1)))

    lam = (jnp.exp(jnp.sum(lq1_ref[...] * lk1_ref[...], axis=1, keepdims=True))
           - jnp.exp(jnp.sum(lq2_ref[...] * lk2_ref[...], axis=1, keepdims=True))
           + lambda_init)
    o_t = acc_ref[0] / l_ref[0] - lam * (acc_ref[1] / l_ref[1])
    ms = jnp.mean(o_t * o_t, axis=0, keepdims=True)
    o_t = o_t * lax.rsqrt(ms + HEAD_NORM_EPS) * (g_ref[...] * (1.0 - lambda_init))
    o_ref[0] = o_t.T.astype(o_ref.dtype)


def _alibi_features(slope2, blk):
    pieces = []
    rest = slope2
    for _ in range(ALIBI_SLOPE_PIECES):
        piece = rest.astype(BF16)
        pieces.append(piece)
        rest = rest - piece.astype(F32)
    qf = jnp.stack([p * r for p in pieces for r in (ALIBI_POS_RADIX, 1)], axis=1)
    pos = jnp.arange(blk)
    digits = jnp.stack([pos // ALIBI_POS_RADIX, pos % ALIBI_POS_RADIX] * ALIBI_SLOPE_PIECES, axis=1)
    pad = HEAD_DIM - 2 * ALIBI_SLOPE_PIECES
    qf = jnp.pad(qf.astype(BF16), ((0, 0), (0, pad)))[:, None, :]
    kf = jnp.pad(digits.astype(BF16), ((0, 0), (0, pad)))
    return qf, kf


def _diff_attention(q, k, vt, lq1, lk1, lq2, lk2, g_head, *, layer_idx, tq):
    b, s, d = q.shape
    tk = vt.shape[3]
    heads = d // DIFF_V_DIM
    qc = min(tq, V7X_MXUS * V7X_MXU_DIM)
    assert tk <= ALIBI_POS_RADIX * 256
    lambda_init = 0.8 - 0.6 * math.exp(-0.3 * layer_idx)
    slope2 = LOG2_E * 2.0 ** (-8.0 * jnp.arange(1, heads + 1, dtype=F32) / heads)
    qf, kf = _alibi_features(slope2, tk)
    vec = lambda a: a.reshape(1, -1).astype(F32)
    est = (2 * s * (2 * HEAD_DIM + DIFF_V_DIM) * 2 + 12 * 2 * (tq // qc) * tk * qc * 4
           + 4 * tq * DIFF_V_DIM * 4)
    return pl.pallas_call(
        functools.partial(_diff_kernel, tq=tq, tk=tk, qc=qc, lambda_init=lambda_init),
        grid=(b, heads, s // tq),
        in_specs=[
            pl.BlockSpec(memory_space=pltpu.SMEM),
            _const_spec((1, HEAD_DIM)), _const_spec((1, HEAD_DIM)),
            _const_spec((1, HEAD_DIM)), _const_spec((1, HEAD_DIM)),
            _const_spec((DIFF_V_DIM, 1)),
            pl.BlockSpec((1, 1, HEAD_DIM), lambda bi, h, i: (h, 0, 0)),
            _const_spec((tk, HEAD_DIM)),
            pl.BlockSpec((1, tq, HEAD_DIM), lambda bi, h, i: (bi, i, 2 * h)),
            pl.BlockSpec((1, tq, HEAD_DIM), lambda bi, h, i: (bi, i, 2 * h + 1)),
            pl.BlockSpec((1, s, HEAD_DIM), lambda bi, h, i: (bi, 0, 2 * h)),
            pl.BlockSpec((1, s, HEAD_DIM), lambda bi, h, i: (bi, 0, 2 * h + 1)),
            pl.BlockSpec((1, s // tk, DIFF_V_DIM, tk), lambda bi, h, i: (bi, 0, h, 0)),
        ],
        out_specs=pl.BlockSpec((1, tq, DIFF_V_DIM), lambda bi, h, i: (bi, i, h)),
        out_shape=jax.ShapeDtypeStruct((b, s, d), BF16),
        scratch_shapes=[pltpu.VMEM((2, 1, tq), F32), pltpu.VMEM((2, 1, tq), F32),
                        pltpu.VMEM((2, DIFF_V_DIM, tq), F32),
                        pltpu.VMEM((2, 2 * (tq // qc), tk, qc), F32),
                        pltpu.SMEM((2, s // tk), F32)],
        compiler_params=pltpu.CompilerParams(
            dimension_semantics=("arbitrary", "arbitrary", "arbitrary"),
            vmem_limit_bytes=_vmem_limit(est)),
        name="diff_attention",
    )(slope2, vec(lq1), vec(lk1), vec(lq2), vec(lk2), g_head.reshape(-1, 1).astype(F32), qf, kf,
      q, q, k, k, vt)


def _post_kernel(mix_ref, mq_ref, x_ref, km_ref, vm_ref, wo_ref, g_ref, w1_ref, w2_ref, gf_ref,
                 o_ref, *, ff_chunks, final_norm):
    tm, d = x_ref.shape
    mw = mq_ref.shape[1]
    d_ff = w2_ref.shape[0]
    head_dim = mw // MEM_HEADS

    mq = mq_ref[...]
    km = km_ref[0]
    vm = vm_ref[0]
    lane_head = lax.broadcasted_iota(jnp.int32, (tm, mw), 1) // head_dim
    mo = jnp.zeros((tm, mw), F32)
    for hd in range(MEM_HEADS):
        in_head = lane_head == hd
        s = _dot_nt(jnp.where(in_head, mq, jnp.zeros_like(mq)), km)
        p = jnp.exp(s - jnp.max(s, axis=1, keepdims=True))
        l = jnp.sum(p, axis=1, keepdims=True)
        mo = jnp.where(in_head, _dot(p.astype(BF16), vm) / l, mo)

    y = x_ref[...] + _dot(mix_ref[...], wo_ref[0:d, :]) + _dot(mo.astype(BF16), wo_ref[d:, :])

    yn = _rms(y, g_ref[...], NORM_EPS).astype(BF16)
    hidden = []
    for lo, hi in ff_chunks:
        gate = _dot(yn, w1_ref[:, lo:hi])
        up = _dot(yn, w1_ref[:, d_ff + lo:d_ff + hi])
        hidden.append((gate * (1.0 / (1.0 + jnp.exp(-gate))) * up).astype(BF16))
    y = y + _dot(jnp.concatenate(hidden, axis=1), w2_ref[...])
    if final_norm:
        y = _rms(y, gf_ref[...], NORM_EPS)
    o_ref[...] = y


def _ff_chunks(d_ff, n_chunks, align):
    tiles = d_ff // align
    bounds = [align * ((tiles * c) // n_chunks) for c in range(n_chunks)] + [d_ff]
    return tuple((bounds[c], bounds[c + 1]) for c in range(n_chunks))


def _post(mix2d, mq2d, x2d, km, vm, wo, g_ffn, w1, w2, g_final, *, seq, tm, final_norm):
    t, d = x2d.shape
    mw = mq2d.shape[1]
    mem_len = km.shape[1]
    d_ff = w2.shape[0]
    tiles_per_seq = seq // tm
    ff_chunks = _ff_chunks(d_ff, 2, 256)
    chunk = max(hi - lo for lo, hi in ff_chunks)
    est = ((wo.size + w1.size + w2.size) * 2 + 2 * tm * (d + mw) * 2 + 4 * tm * d * 4
           + tm * chunk * 12 + 4 * tm * d * 4)
    return pl.pallas_call(
        functools.partial(_post_kernel, ff_chunks=ff_chunks, final_norm=final_norm),
        grid=(t // tm,),
        in_specs=[
            pl.BlockSpec((tm, d), lambda i: (i, 0)),
            pl.BlockSpec((tm, mw), lambda i: (i, 0)),
            pl.BlockSpec((tm, d), lambda i: (i, 0)),
            pl.BlockSpec((1, mem_len, mw), lambda i: (i // tiles_per_seq, 0, 0)),
            pl.BlockSpec((1, mem_len, mw), lambda i: (i // tiles_per_seq, 0, 0)),
            _const_spec(wo.shape),
            _const_spec((1, d)),
            _const_spec(w1.shape),
            _const_spec(w2.shape),
            _const_spec((1, d)),
        ],
        out_specs=pl.BlockSpec((tm, d), lambda i: (i, 0)),
        out_shape=jax.ShapeDtypeStruct((t, d), F32),
        compiler_params=pltpu.CompilerParams(
            dimension_semantics=("arbitrary",), vmem_limit_bytes=_vmem_limit(est)),
        name="post",
    )(mix2d, mq2d, x2d, km, vm, wo, g_ffn.reshape(1, d), w1, w2, g_final.reshape(1, d))


def kernel(x, mem, g_mix, w_in, w_mem_kv, w_o, g_ffn, w_ffn_in, w_ffn_out,
           lam_q1, lam_k1, lam_q2, lam_k2, g_diff_head, conv_w, g_mem, g_final):
    b, seq, d = x.shape
    mem_len = mem.shape[1]
    depth = w_in.shape[0]
    t = b * seq
    tm = min(512, seq)
    sb_tq = min(1024, seq)
    diff_tq = min(1024, seq)

    km_all, vm_all = _mem_kv(mem.reshape(b * mem_len, d), g_mem, w_mem_kv.astype(BF16))
    mw = km_all.shape[-1]
    km_all = km_all.reshape(depth, b, mem_len, mw)
    vm_all = vm_all.reshape(depth, b, mem_len, mw)

    x2d = x.reshape(t, d)
    for i in range(depth):
        kind = i % N_MIXERS
        j = i // N_MIXERS
        w = w_in[i].astype(BF16)
        if kind == 2:
            mix, mq = _conv_proj(x2d, g_mix[i], w, conv_w[j], seq=seq, tm=tm)
        else:
            q, k, vt, mq = _in_proj(x2d, g_mix[i], w, q_scale=LOG2_E * HEAD_DIM ** -0.5, tm=tm)
            q, k = (a.reshape(b, seq, d) for a in (q, k))
            vt = vt.reshape(b, seq // tm, d, tm)
            if kind == 0:
                mix = _sb_attention(q, k, vt, tq=sb_tq)
            else:
                mix = _diff_attention(q, k, vt, lam_q1[j], lam_k1[j], lam_q2[j], lam_k2[j],
                                      g_diff_head[j], layer_idx=i, tq=diff_tq)
            mix = mix.reshape(t, d)
        x2d = _post(mix, mq, x2d, km_all[i], vm_all[i], w_o[i].astype(BF16), g_ffn[i],
                    w_ffn_in[i].astype(BF16), w_ffn_out[i].astype(BF16), g_final,
                    seq=seq, tm=tm, final_norm=(i == depth - 1))
    return x2d.reshape(b, seq, d)
```
